```python
import math
import jax, jax.numpy as jnp
from jax import lax
import numpy as np

D_MODEL = 2048
BATCH = 8
SEQ = 4096
DEPTH = 1
DEC_BATCH = 16
DEC_SEQ = 2048
PAST_LEN = 128

N_HEADS_A = 8
HEAD_DIM_A = 128
WIDTH_A = N_HEADS_A * HEAD_DIM_A
DILATED_CONFIGS = ((128, 1), (512, 4), (2048, 16))
N_HEADS_B = 8
DIFF_QK_DIM = 64
DIFF_V_DIM = 2 * DIFF_QK_DIM
WIDTH_B = N_HEADS_B * DIFF_V_DIM
Q_BLOCK = 128
MIX_WIDTH = WIDTH_A + WIDTH_B
PROJ_WIDTH = 3 * WIDTH_A + 3 * WIDTH_B
D_FF = ((8 * D_MODEL // 3 + 255) // 256) * 256
N_BUCKETS = 32
MAX_DISTANCE = 1024
N_BIAS_HEADS = N_HEADS_A + N_HEADS_B
EPS = 1e-6
NEG_INF = -1e30

kernel_name = "hymba_dilated_diff_encoder"


def rms_norm(x, g):
    xf = x.astype(jnp.float32)
    y = xf * lax.rsqrt(jnp.mean(xf * xf, axis=-1, keepdims=True) + EPS)
    return (y * g.astype(jnp.float32)).astype(x.dtype)


def rel_bucket(rel):
    half = N_BUCKETS // 2
    max_exact = half // 2
    n = jnp.abs(rel)
    nf = jnp.maximum(n, 1).astype(jnp.float32)
    large = max_exact + (jnp.log(nf / max_exact) / math.log(MAX_DISTANCE / max_exact)
                         * (half - max_exact)).astype(jnp.int32)
    large = jnp.minimum(large, half - 1)
    return jnp.where(rel > 0, half, 0) + jnp.where(n < max_exact, n, large)


def dilated_branch(q, k, v, rel_bias, window, dilation):
    B, S, H, Dh = q.shape
    half = window // (2 * dilation)
    blk = half
    L = S // dilation
    nb = -(-L // blk)
    Lp = nb * blk

    def to_phases(t):
        return t.reshape(B, L, dilation, H, Dh).transpose(0, 2, 1, 3, 4).reshape(B * dilation, L, H, Dh)

    qs = jnp.pad(to_phases(q), ((0, 0), (0, Lp - L), (0, 0), (0, 0))).reshape(B * dilation, nb, blk, H, Dh)
    pad_kv = ((0, 0), (blk, Lp - L + blk), (0, 0), (0, 0))

    def windows(t):
        tb = jnp.pad(to_phases(t), pad_kv).reshape(B * dilation, nb + 2, blk, H, Dh)
        return jnp.concatenate([tb[:, :-2], tb[:, 1:-1], tb[:, 2:]], axis=2)

    kw = windows(k)
    vw = windows(v).astype(jnp.float32)
    kpos = jnp.arange(-blk, Lp + blk).reshape(nb + 2, blk)
    kpos_w = jnp.concatenate([kpos[:-2], kpos[1:-1], kpos[2:]], axis=1)
    kvalid = (kpos_w >= 0) & (kpos_w < L)
    rel = jnp.arange(3 * blk)[None, :] - blk - jnp.arange(blk)[:, None]
    mask = (jnp.abs(rel) <= half)[None] & kvalid[:, None, :]
    bias = rel_bias[rel_bucket(rel * dilation)][..., :N_HEADS_A].transpose(2, 0, 1)
    scale = 1.0 / math.sqrt(Dh)
    s = jnp.einsum('znqhd,znkhd->znhqk', qs, kw, preferred_element_type=jnp.float32) * scale
    s = s + bias.astype(jnp.float32)
    s = jnp.where(mask[None, :, None], s, NEG_INF)
    m = jnp.max(s, axis=-1, keepdims=True)
    p = jnp.exp(s - m)
    l = jnp.sum(p, axis=-1)
    o = jnp.einsum('znhqk,znkhd->znqhd', p, vw) / jnp.moveaxis(l, 2, 3)[..., None]
    m = jnp.moveaxis(m[..., 0], 2, 3)
    l = jnp.moveaxis(l, 2, 3)

    def from_phases(t):
        rest = t.shape[3:]
        t = t.reshape((B, dilation, Lp) + rest)[:, :, :L]
        t = jnp.swapaxes(t, 1, 2)
        return t.reshape((B, S) + rest)

    return from_phases(o), from_phases(m), from_phases(l)


def dilated_mixture(q, k, v, rel_bias):
    outs = [dilated_branch(q, k, v, rel_bias, w, d) for (w, d) in DILATED_CONFIGS]
    o_all = jnp.stack([o for o, _, _ in outs])
    m_all = jnp.stack([m for _, m, _ in outs])
    l_all = jnp.stack([l for _, _, l in outs])
    wts = l_all * jnp.exp(m_all - jnp.max(m_all, axis=0, keepdims=True))
    return jnp.sum(wts[..., None] * o_all, axis=0) / jnp.sum(wts, axis=0)[..., None]


def diff_attention(q, k, v, lam, rel_bias):
    B, S, H = q.shape[:3]
    nq = S // Q_BLOCK
    qb = q.reshape(B, nq, Q_BLOCK, H, 2, DIFF_QK_DIM).transpose(1, 0, 2, 3, 4, 5)
    starts = jnp.arange(nq) * Q_BLOCK
    kpos = jnp.arange(S)
    vf = v.astype(jnp.float32)
    scale = 1.0 / math.sqrt(DIFF_QK_DIM)

    def block(args):
        qblk, q0 = args
        s = jnp.einsum('bqhcd,bkhcd->bhcqk', qblk, k, preferred_element_type=jnp.float32) * scale
        rel = kpos[None, :] - (q0 + jnp.arange(Q_BLOCK))[:, None]
        bias = rel_bias[rel_bucket(rel)][..., N_HEADS_A:].transpose(2, 0, 1)
        p = jax.nn.softmax(s + bias.astype(jnp.float32)[None, :, None], axis=-1)
        a = p[:, :, 0] - lam * p[:, :, 1]
        return jnp.einsum('bhqk,bkhd->bqhd', a, vf)

    o = lax.map(block, (qb, starts))
    return o.transpose(1, 0, 2, 3, 4).reshape(B, S, H, DIFF_V_DIM)


def encoder_layer(x, layer, pre_mix_g, post_mix_g, pre_ffn_g, post_ffn_g, w_in, w_out, norm_a_g,
                  lambda_q1, lambda_k1, lambda_q2, lambda_k2, subln_g, w_gate, w_up, w_down, rel_bias):
    B, S, _ = x.shape
    h = rms_norm(x, pre_mix_g)
    proj = jnp.einsum('bsd,de->bse', h, w_in)
    qa, ka, va, qb, kb, vb = jnp.split(proj, 6, axis=-1)
    qa = qa.reshape(B, S, N_HEADS_A, HEAD_DIM_A)
    ka = ka.reshape(B, S, N_HEADS_A, HEAD_DIM_A)
    va = va.reshape(B, S, N_HEADS_A, HEAD_DIM_A)
    oa = dilated_mixture(qa, ka, va, rel_bias).reshape(B, S, WIDTH_A)
    oa = rms_norm(oa, norm_a_g)

    qb = qb.reshape(B, S, N_HEADS_B, 2, DIFF_QK_DIM)
    kb = kb.reshape(B, S, N_HEADS_B, 2, DIFF_QK_DIM)
    vb = vb.reshape(B, S, N_HEADS_B, DIFF_V_DIM)
    lambda_init = 0.8 - 0.6 * math.exp(-0.3 * layer)
    lam = (jnp.exp(jnp.sum(lambda_q1.astype(jnp.float32) * lambda_k1.astype(jnp.float32)))
           - jnp.exp(jnp.sum(lambda_q2.astype(jnp.float32) * lambda_k2.astype(jnp.float32)))
           + lambda_init)
    ob = diff_attention(qb, kb, vb, lam, rel_bias)
    ob = rms_norm(ob, subln_g) * (1.0 - lambda_init)
    ob = ob.reshape(B, S, WIDTH_B)

    mixed = jnp.concatenate([oa.astype(jnp.float32), ob], axis=-1).astype(x.dtype)
    y = jnp.einsum('bse,ed->bsd', mixed, w_out)
    x = x + rms_norm(y, post_mix_g)
    h = rms_norm(x, pre_ffn_g)
    f = jax.nn.silu(jnp.einsum('bsd,df->bsf', h, w_gate)) * jnp.einsum('bsd,df->bsf', h, w_up)
    f = jnp.einsum('bsf,fd->bsd', f, w_down)
    return x + rms_norm(f, post_ffn_g)


def trunk(x, pre_mix_g, post_mix_g, pre_ffn_g, post_ffn_g, w_in, w_out, norm_a_g,
          lambda_q1, lambda_k1, lambda_q2, lambda_k2, subln_g, w_gate, w_up, w_down, rel_bias):
    for layer in range(DEPTH):
        x = encoder_layer(x, layer, pre_mix_g[layer], post_mix_g[layer], pre_ffn_g[layer], post_ffn_g[layer],
                          w_in[layer], w_out[layer], norm_a_g[layer], lambda_q1[layer], lambda_k1[layer],
                          lambda_q2[layer], lambda_k2[layer], subln_g[layer], w_gate[layer], w_up[layer],
                          w_down[layer], rel_bias)
    return x


def setup_inputs(seed: int = 0) -> dict:
    key = jax.random.key(seed)
    ks = jax.random.split(key, 20)
    f32 = jnp.float32

    def gain(k, n):
        return 1.0 + 0.02 * jax.random.normal(k, (DEPTH, n), f32)

    return {
        "x_prompt": jax.random.normal(ks[0], (BATCH, SEQ, D_MODEL), f32),
        "x_sample": jax.random.normal(ks[1], (DEC_BATCH, DEC_SEQ, D_MODEL), f32),
        "pre_mix_g": gain(ks[2], D_MODEL),
        "post_mix_g": gain(ks[3], D_MODEL),
        "pre_ffn_g": gain(ks[4], D_MODEL),
        "post_ffn_g": gain(ks[5], D_MODEL),
        "w_in": jax.random.normal(ks[6], (DEPTH, D_MODEL, PROJ_WIDTH), f32) * D_MODEL ** -0.5,
        "w_out": jax.random.normal(ks[7], (DEPTH, MIX_WIDTH, D_MODEL), f32) * MIX_WIDTH ** -0.5,
        "norm_a_g": gain(ks[8], WIDTH_A),
        "lambda_q1": 0.1 * jax.random.normal(ks[9], (DEPTH, DIFF_QK_DIM), f32),
        "lambda_k1": 0.1 * jax.random.normal(ks[10], (DEPTH, DIFF_QK_DIM), f32),
        "lambda_q2": 0.1 * jax.random.normal(ks[11], (DEPTH, DIFF_QK_DIM), f32),
        "lambda_k2": 0.1 * jax.random.normal(ks[12], (DEPTH, DIFF_QK_DIM), f32),
        "subln_g": gain(ks[13], DIFF_V_DIM),
        "w_gate": jax.random.normal(ks[14], (DEPTH, D_MODEL, D_FF), f32) * D_MODEL ** -0.5,
        "w_up": jax.random.normal(ks[15], (DEPTH, D_MODEL, D_FF), f32) * D_MODEL ** -0.5,
        "w_down": jax.random.normal(ks[16], (DEPTH, D_FF, D_MODEL), f32) * D_FF ** -0.5,
        "rel_bias": 0.2 * jax.random.normal(ks[17], (N_BUCKETS, N_BIAS_HEADS), f32),
    }


def reference(x_prompt, x_sample, pre_mix_g, post_mix_g, pre_ffn_g, post_ffn_g, w_in, w_out, norm_a_g,
              lambda_q1, lambda_k1, lambda_q2, lambda_k2, subln_g, w_gate, w_up, w_down, rel_bias):
    y_prompt = trunk(x_prompt, pre_mix_g, post_mix_g, pre_ffn_g, post_ffn_g, w_in, w_out, norm_a_g,
                     lambda_q1, lambda_k1, lambda_q2, lambda_k2, subln_g, w_gate, w_up, w_down, rel_bias)
    y_sample = trunk(x_sample, pre_mix_g, post_mix_g, pre_ffn_g, post_ffn_g, w_in, w_out, norm_a_g,
                     lambda_q1, lambda_k1, lambda_q2, lambda_k2, subln_g, w_gate, w_up, w_down, rel_bias)
    return (y_prompt, y_sample)
```

```python
import functools
import math

import jax
import jax.numpy as jnp
from jax import lax
from jax.experimental import pallas as pl
from jax.experimental.pallas import tpu as pltpu

F32 = jnp.float32
BF16 = jnp.bfloat16

EPS = 1e-6
NEG_INF = -1e30
HEAD_DIM = 128
DIFF_QK_DIM = 64
DILATED_CONFIGS = ((128, 1), (512, 4), (2048, 16))
MAX_DISTANCE = 1024
LANES = 128
VMEM_LIMIT = 56 * 1024 * 1024

DIFF_BIAS_DMAX = -(-(MAX_DISTANCE + LANES - 1) // LANES)


def _pick(n, candidates):
    for c in candidates:
        if n % c == 0:
            return c
    raise ValueError(f"no tile in {candidates} divides {n}")


def _rms(x, g):
    ms = jnp.mean(x * x, axis=-1, keepdims=True)
    return x * lax.rsqrt(ms + EPS) * g


def _rel_bucket(rel, n_buckets):
    half = n_buckets // 2
    max_exact = half // 2
    n = jnp.abs(rel)
    nf = jnp.maximum(n, 1).astype(F32)
    large = max_exact + (jnp.log(nf / max_exact) / math.log(MAX_DISTANCE / max_exact)
                         * (half - max_exact)).astype(jnp.int32)
    large = jnp.minimum(large, half - 1)
    return jnp.where(rel > 0, half, 0) + jnp.where(n < max_exact, n, large)


def _in_proj_kernel(x_ref, g_ref, w_ref, o_ref, h_ref):
    @pl.when(pl.program_id(1) == 0)
    def _():
        h_ref[...] = _rms(x_ref[...], g_ref[...]).astype(BF16)

    o_ref[...] = jnp.dot(h_ref[...], w_ref[...], preferred_element_type=F32).astype(o_ref.dtype)


def _in_proj(x2, g, w):
    t, d = x2.shape
    n = w.shape[1]
    tm = _pick(t, (1024, 512, 256, 128))
    tn = _pick(n, (1024, 768, 512, 384, 256, 128))
    return pl.pallas_call(
        _in_proj_kernel,
        grid=(t // tm, n // tn),
        in_specs=[
            pl.BlockSpec((tm, d), lambda i, j: (i, 0)),
            pl.BlockSpec((1, d), lambda i, j: (0, 0)),
            pl.BlockSpec((d, tn), lambda i, j: (0, j)),
        ],
        out_specs=pl.BlockSpec((tm, tn), lambda i, j: (i, j)),
        out_shape=jax.ShapeDtypeStruct((t, n), BF16),
        scratch_shapes=[pltpu.VMEM((tm, d), BF16)],
        compiler_params=pltpu.CompilerParams(
            dimension_semantics=("parallel", "arbitrary"), vmem_limit_bytes=VMEM_LIMIT),
        name="in_proj",
    )(x2, g, w)


def _dilated_kernel(q_ref, k_ref, v_ref, bias_ref, o_ref, qf, kf, vf, acc, m_st, l_st, *, configs, pad):
    s_len = q_ref.shape[1]
    scale = 1.0 / math.sqrt(HEAD_DIM)

    qf[...] = q_ref[0].astype(F32)
    zeros = jnp.zeros((pad, HEAD_DIM), F32)
    for buf, src in ((kf, k_ref), (vf, v_ref)):
        buf[pl.ds(0, pad), :] = zeros
        buf[pl.ds(pad + s_len, pad), :] = zeros
        buf[pl.ds(pad, s_len), :] = src[0].astype(F32)

    for ci, (window, dil) in enumerate(configs):
        half = window // (2 * dil)
        blk = 2 * half
        nkeys = 4 * half
        sub_len = s_len // dil
        shift = dil.bit_length() - 1

        def rows(start, size, dil=dil):
            if dil == 1:
                return pl.ds(pl.multiple_of(start, 8), size)
            return pl.ds(start, size, stride=dil)

        def body(it, carry, ci=ci, dil=dil, half=half, blk=blk, nkeys=nkeys, sub_len=sub_len,
                 shift=shift, rows=rows):
            phase = it & (dil - 1)
            n = it >> shift
            sub0 = n * blk
            q_rows = rows(phase + dil * sub0, blk)
            kv_rows = rows(pad + phase + dil * (sub0 - half), nkeys)
            qb = qf[q_rows, :].astype(BF16)
            kb = kf[kv_rows, :].astype(BF16)
            vb = vf[kv_rows, :].astype(BF16)
            s = lax.dot_general(qb, kb, (((1,), (1,)), ((), ())), preferred_element_type=F32)
            key_sub = sub0 - half + lax.broadcasted_iota(jnp.int32, (1, nkeys), 1)
            valid = jnp.where((key_sub >= 0) & (key_sub < sub_len), 0.0, NEG_INF)
            s = s * scale + bias_ref[0, ci] + valid
            m_cur = jnp.max(s, axis=1, keepdims=True)
            if ci == 0:
                m_new = jnp.broadcast_to(m_cur, (blk, LANES))
            else:
                m_old = m_st[q_rows, :]
                m_new = jnp.maximum(m_old, m_cur)
            p = jnp.exp(s - jnp.concatenate([m_new] * (nkeys // LANES), axis=1))
            l_new = jnp.sum(p, axis=1, keepdims=True)
            a_new = jnp.dot(p.astype(BF16), vb, preferred_element_type=F32)
            if ci == 0:
                l_new = jnp.broadcast_to(l_new, (blk, LANES))
            else:
                alpha = jnp.exp(m_old - m_new)
                l_new = alpha * l_st[q_rows, :] + l_new
                a_new = alpha * acc[q_rows, :] + a_new
            m_st[q_rows, :] = m_new
            l_st[q_rows, :] = l_new
            acc[q_rows, :] = a_new
            return carry

        lax.fori_loop(0, s_len // blk, body, 0)

    o_ref[0] = (acc[...] / l_st[...]).astype(o_ref.dtype)


def _dilated_bias(rel_bias, n_heads):
    tiles = []
    for window, dil in DILATED_CONFIGS:
        half = window // (2 * dil)
        rel = (jnp.arange(4 * half)[None, :] - half) - jnp.arange(2 * half)[:, None]
        b = rel_bias[_rel_bucket(rel * dil, rel_bias.shape[0])][..., :n_heads]
        tiles.append(jnp.where((jnp.abs(rel) <= half)[..., None], b.astype(F32), NEG_INF))
    return jnp.stack(tiles).transpose(3, 0, 1, 2)


def _dilated_attn(proj3, bias, n_heads):
    b, s_len, _ = proj3.shape
    halves = {w // (2 * d) for w, d in DILATED_CONFIGS}
    assert len(halves) == 1, "one bias tile shape for all branches"
    half = halves.pop()
    assert 2 * half == LANES, "state rows are stored one 128-lane slab per query block"
    pad = half * max(d for _, d in DILATED_CONFIGS)
    for _, d in DILATED_CONFIGS:
        assert s_len % (2 * half * d) == 0 and d & (d - 1) == 0
    kern = functools.partial(_dilated_kernel, configs=DILATED_CONFIGS, pad=pad)
    col = lambda off: (lambda bi, h: (bi, 0, off + h))
    return pl.pallas_call(
        kern,
        grid=(b, n_heads),
        in_specs=[
            pl.BlockSpec((1, s_len, HEAD_DIM), col(0)),
            pl.BlockSpec((1, s_len, HEAD_DIM), col(n_heads)),
            pl.BlockSpec((1, s_len, HEAD_DIM), col(2 * n_heads)),
            pl.BlockSpec((1,) + bias.shape[1:], lambda bi, h: (h, 0, 0, 0)),
        ],
        out_specs=pl.BlockSpec((1, s_len, HEAD_DIM), lambda bi, h: (bi, 0, h)),
        out_shape=jax.ShapeDtypeStruct((b, s_len, n_heads * HEAD_DIM), F32),
        scratch_shapes=[
            pltpu.VMEM((s_len, HEAD_DIM), F32),
            pltpu.VMEM((s_len + 2 * pad, HEAD_DIM), F32),
            pltpu.VMEM((s_len + 2 * pad, HEAD_DIM), F32),
            pltpu.VMEM((s_len, HEAD_DIM), F32),
            pltpu.VMEM((s_len, LANES), F32),
            pltpu.VMEM((s_len, LANES), F32),
        ],
        compiler_params=pltpu.CompilerParams(
            dimension_semantics=("parallel", "parallel"), vmem_limit_bytes=VMEM_LIMIT),
        name="dilated_attn",
    )(proj3, proj3, proj3, bias)


def _diff_attn_kernel(q_ref, k_ref, v_ref, bias_ref, lq1_ref, lk1_ref, lq2_ref, lk2_ref, g_ref,
                      o_ref, qs_ref, m_ref, l_ref, acc_ref, *, tq, tk, lambda_init):
    qi = pl.program_id(2)
    s_len = k_ref.shape[1]
    qsub, ksub = tq // LANES, tk // LANES

    q = q_ref[0] * (1.0 / math.sqrt(DIFF_QK_DIM))
    lane = lax.broadcasted_iota(jnp.int32, (tq, HEAD_DIM), 1)
    qs_ref[pl.ds(0, tq), :] = jnp.where(lane < DIFF_QK_DIM, q, 0).astype(BF16)
    qs_ref[pl.ds(tq, tq), :] = jnp.where(lane >= DIFF_QK_DIM, q, 0).astype(BF16)
    m_ref[...] = jnp.full(m_ref.shape, NEG_INF, F32)
    l_ref[...] = jnp.zeros(l_ref.shape, F32)
    acc_ref[...] = jnp.zeros(acc_ref.shape, F32)

    def body(kb, carry):
        k0 = pl.multiple_of(kb * tk, tk)
        kt = k_ref[0, pl.ds(k0, tk), :]
        vt = v_ref[0, pl.ds(k0, tk), :]
        s = lax.dot_general(qs_ref[...], kt, (((1,), (1,)), ((), ())), preferred_element_type=F32)
        bias_rows = []
        for a in range(qsub):
            tiles = []
            for c in range(ksub):
                diff = (kb * ksub + c) - (qi * qsub + a)
                idx = jnp.minimum(jnp.maximum(diff, -DIFF_BIAS_DMAX), DIFF_BIAS_DMAX) + DIFF_BIAS_DMAX
                tiles.append(bias_ref[0, idx])
            bias_rows.append(jnp.concatenate(tiles, axis=1))
        bias = jnp.concatenate(bias_rows, axis=0)
        s = s + jnp.concatenate([bias, bias], axis=0)

        m_prev = m_ref[...]
        m_next = jnp.maximum(m_prev, jnp.max(s, axis=1, keepdims=True))
        alpha = jnp.exp(m_prev - m_next)
        p = jnp.exp(s - jnp.concatenate([m_next] * ksub, axis=1))
        l_ref[...] = alpha * l_ref[...] + jnp.sum(p, axis=1, keepdims=True)
        acc_ref[...] = alpha * acc_ref[...] + jnp.dot(p.astype(BF16), vt, preferred_element_type=F32)
        m_ref[...] = m_next
        return carry

    lax.fori_loop(0, s_len // tk, body, 0)

    o = acc_ref[...] / l_ref[...]
    lam = (jnp.exp(jnp.sum(lq1_ref[...] * lk1_ref[...], axis=-1, keepdims=True))
           - jnp.exp(jnp.sum(lq2_ref[...] * lk2_ref[...], axis=-1, keepdims=True))
           + lambda_init)
    od = o[:tq] - lam * o[tq:]
    o_ref[0] = (_rms(od, g_ref[...]) * (1.0 - lambda_init)).astype(o_ref.dtype)


def _diff_bias(rel_bias, n_heads_a, n_heads_b):
    t = jnp.arange(2 * DIFF_BIAS_DMAX + 1) - DIFF_BIAS_DMAX
    rel = (LANES * t[:, None, None] + jnp.arange(LANES)[None, None, :] - jnp.arange(LANES)[None, :, None])
    b = rel_bias[_rel_bucket(rel, rel_bias.shape[0])][..., n_heads_a:n_heads_a + n_heads_b]
    return b.astype(F32).transpose(3, 0, 1, 2)


def _diff_attn(proj3, bias, lams, subln_g, n_heads_a, n_heads_b, lambda_init):
    b, s_len, _ = proj3.shape
    tq = _pick(s_len, (256, 128))
    tk = _pick(s_len, (512, 256, 128))
    base = 3 * n_heads_a
    kern = functools.partial(_diff_attn_kernel, tq=tq, tk=tk, lambda_init=lambda_init)
    vec = lambda n: pl.BlockSpec((1, n), lambda bi, h, qi: (0, 0))
    return pl.pallas_call(
        kern,
        grid=(b, n_heads_b, s_len // tq),
        in_specs=[
            pl.BlockSpec((1, tq, HEAD_DIM), lambda bi, h, qi: (bi, qi, base + h)),
            pl.BlockSpec((1, s_len, HEAD_DIM), lambda bi, h, qi: (bi, 0, base + n_heads_b + h)),
            pl.BlockSpec((1, s_len, HEAD_DIM), lambda bi, h, qi: (bi, 0, base + 2 * n_heads_b + h)),
            pl.BlockSpec((1,) + bias.shape[1:], lambda bi, h, qi: (h, 0, 0, 0)),
            vec(DIFF_QK_DIM), vec(DIFF_QK_DIM), vec(DIFF_QK_DIM), vec(DIFF_QK_DIM),
            vec(HEAD_DIM),
        ],
        out_specs=pl.BlockSpec((1, tq, HEAD_DIM), lambda bi, h, qi: (bi, qi, h)),
        out_shape=jax.ShapeDtypeStruct((b, s_len, n_heads_b * HEAD_DIM), BF16),
        scratch_shapes=[
            pltpu.VMEM((2 * tq, HEAD_DIM), BF16),
            pltpu.VMEM((2 * tq, LANES), F32),
            pltpu.VMEM((2 * tq, LANES), F32),
            pltpu.VMEM((2 * tq, HEAD_DIM), F32),
        ],
        compiler_params=pltpu.CompilerParams(
            dimension_semantics=("parallel", "parallel", "arbitrary"), vmem_limit_bytes=VMEM_LIMIT),
        name="diff_attn",
    )(proj3, proj3, proj3, bias, *lams, subln_g)


def _out_proj_kernel(oa_ref, ob_ref, x_ref, wa_ref, wb_ref, ga_ref, gp_ref, o_ref):
    oa = _rms(oa_ref[...], ga_ref[...]).astype(BF16)
    y = jnp.dot(oa, wa_ref[...], preferred_element_type=F32)
    y = y + jnp.dot(ob_ref[...], wb_ref[...], preferred_element_type=F32)
    o_ref[...] = x_ref[...] + _rms(y, gp_ref[...])


def _out_proj(oa2, ob2, x2, wa, wb, ga, gp):
    t, d = x2.shape
    wa_w, wb_w = wa.shape[0], wb.shape[0]
    tm = _pick(t, (512, 256, 128))
    const = lambda i: (0, 0)
    row = lambda i: (i, 0)
    return pl.pallas_call(
        _out_proj_kernel,
        grid=(t // tm,),
        in_specs=[
            pl.BlockSpec((tm, wa_w), row),
            pl.BlockSpec((tm, wb_w), row),
            pl.BlockSpec((tm, d), row),
            pl.BlockSpec((wa_w, d), const),
            pl.BlockSpec((wb_w, d), const),
            pl.BlockSpec((1, wa_w), const),
            pl.BlockSpec((1, d), const),
        ],
        out_specs=pl.BlockSpec((tm, d), row),
        out_shape=jax.ShapeDtypeStruct((t, d), F32),
        compiler_params=pltpu.CompilerParams(
            dimension_semantics=("parallel",), vmem_limit_bytes=VMEM_LIMIT),
        name="out_proj",
    )(oa2, ob2, x2, wa, wb, ga, gp)


def _ffn_kernel(x_ref, gpre_ref, wg_ref, wu_ref, wd_ref, gpost_ref, o_ref, h_ref, acc_ref):
    j = pl.program_id(1)

    @pl.when(j == 0)
    def _():
        h_ref[...] = _rms(x_ref[...], gpre_ref[...]).astype(BF16)
        acc_ref[...] = jnp.zeros(acc_ref.shape, F32)

    h = h_ref[...]
    gate = jnp.dot(h, wg_ref[...], preferred_element_type=F32)
    up = jnp.dot(h, wu_ref[...], preferred_element_type=F32)
    f = gate * (1.0 / (1.0 + jnp.exp(-gate))) * up
    acc_ref[...] += jnp.dot(f.astype(BF16), wd_ref[...], preferred_element_type=F32)

    @pl.when(j == pl.num_programs(1) - 1)
    def _():
        o_ref[...] = x_ref[...] + _rms(acc_ref[...], gpost_ref[...])


def _ffn(x2, gpre, wg, wu, wd, gpost):
    t, d = x2.shape
    f = wg.shape[1]
    tm = _pick(t, (512, 256, 128))
    tf = _pick(f, (512, 256, 128))
    return pl.pallas_call(
        _ffn_kernel,
        grid=(t // tm, f // tf),
        in_specs=[
            pl.BlockSpec((tm, d), lambda i, j: (i, 0)),
            pl.BlockSpec((1, d), lambda i, j: (0, 0)),
            pl.BlockSpec((d, tf), lambda i, j: (0, j)),
            pl.BlockSpec((d, tf), lambda i, j: (0, j)),
            pl.BlockSpec((tf, d), lambda i, j: (j, 0)),
            pl.BlockSpec((1, d), lambda i, j: (0, 0)),
        ],
        out_specs=pl.BlockSpec((tm, d), lambda i, j: (i, 0)),
        out_shape=jax.ShapeDtypeStruct((t, d), F32),
        scratch_shapes=[pltpu.VMEM((tm, d), BF16), pltpu.VMEM((tm, d), F32)],
        compiler_params=pltpu.CompilerParams(
            dimension_semantics=("parallel", "arbitrary"), vmem_limit_bytes=VMEM_LIMIT),
        name="ffn",
    )(x2, gpre, wg, wu, wd, gpost)


def _encoder_layer(x, layer, p, bias_a, bias_b):
    b, s_len, d = x.shape
    width = p["w_in"].shape[1] // 6
    n_heads = width // HEAD_DIM
    lambda_init = 0.8 - 0.6 * math.exp(-0.3 * layer)
    row = lambda v: v.reshape(1, -1)

    x2 = x.reshape(b * s_len, d)
    proj3 = _in_proj(x2, row(p["pre_mix_g"]), p["w_in"]).reshape(b, s_len, 6 * width)
    oa = _dilated_attn(proj3, bias_a, n_heads)
    lams = [row(p[k]) for k in ("lambda_q1", "lambda_k1", "lambda_q2", "lambda_k2")]
    ob = _diff_attn(proj3, bias_b, lams, row(p["subln_g"]), n_heads, n_heads, lambda_init)
    x1 = _out_proj(oa.reshape(b * s_len, width), ob.reshape(b * s_len, width), x2,
                   p["w_out"][:width], p["w_out"][width:], row(p["norm_a_g"]), row(p["post_mix_g"]))
    y = _ffn(x1, row(p["pre_ffn_g"]), p["w_gate"], p["w_up"], p["w_down"], row(p["post_ffn_g"]))
    return y.reshape(b, s_len, d)


def kernel(x_prompt, x_sample, pre_mix_g, post_mix_g, pre_ffn_g, post_ffn_g, w_in, w_out, norm_a_g,
           lambda_q1, lambda_k1, lambda_q2, lambda_k2, subln_g, w_gate, w_up, w_down, rel_bias):
    depth = w_in.shape[0]
    n_heads = w_in.shape[2] // 6 // HEAD_DIM
    bias_a = _dilated_bias(rel_bias, n_heads)
    bias_b = _diff_bias(rel_bias, n_heads, n_heads)
    layers = []
    for layer in range(depth):
        layers.append(dict(
            pre_mix_g=pre_mix_g[layer], post_mix_g=post_mix_g[layer], pre_ffn_g=pre_ffn_g[layer],
            post_ffn_g=post_ffn_g[layer], norm_a_g=norm_a_g[layer], subln_g=subln_g[layer],
            lambda_q1=lambda_q1[layer], lambda_k1=lambda_k1[layer],
            lambda_q2=lambda_q2[layer], lambda_k2=lambda_k2[layer],
            w_in=w_in[layer].astype(BF16), w_out=w_out[layer].astype(BF16),
            w_gate=w_gate[layer].astype(BF16), w_up=w_up[layer].astype(BF16),
            w_down=w_down[layer].astype(BF16)))
    outs = []
    for x in (x_prompt, x_sample):
        for layer, p in enumerate(layers):
            x = _encoder_layer(x, layer, p, bias_a, bias_b)
        outs.append(x)
    return tuple(outs)
```

```python
import functools
import math

import jax
import jax.numpy as jnp
from jax import lax
from jax.experimental import pallas as pl
from jax.experimental.pallas import tpu as pltpu

F32 = jnp.float32
BF16 = jnp.bfloat16

EPS = 1e-6
NEG_INF = -1e30
HEAD_DIM = 128
DIFF_QK_DIM = 64
DILATED_CONFIGS = ((128, 1), (512, 4), (2048, 16))
MAX_DISTANCE = 1024
LANES = 128
VMEM_LIMIT = 56 * 1024 * 1024

DIFF_BIAS_DMAX = -(-(MAX_DISTANCE + LANES - 1) // LANES)


def _pick(n, candidates):
    for c in candidates:
        if n % c == 0:
            return c
    raise ValueError(f"no tile in {candidates} divides {n}")


def _rms(x, g):
    ms = jnp.mean(x * x, axis=-1, keepdims=True)
    return x * lax.rsqrt(ms + EPS) * g


def _rel_bucket(rel, n_buckets):
    half = n_buckets // 2
    max_exact = half // 2
    n = jnp.abs(rel)
    nf = jnp.maximum(n, 1).astype(F32)
    large = max_exact + (jnp.log(nf / max_exact) / math.log(MAX_DISTANCE / max_exact)
                         * (half - max_exact)).astype(jnp.int32)
    large = jnp.minimum(large, half - 1)
    return jnp.where(rel > 0, half, 0) + jnp.where(n < max_exact, n, large)


def _bias_lookup(table, bucket):
    cols = table.astype(F32).T.reshape((table.shape[1],) + (1,) * bucket.ndim + (table.shape[0],))
    out = jnp.zeros((table.shape[1],) + bucket.shape, F32)
    for b in range(table.shape[0]):
        out = jnp.where(bucket[None] == b, cols[..., b], out)
    return out


def _in_proj_kernel(x_ref, g_ref, w_ref, o_ref, h_ref):
    @pl.when(pl.program_id(1) == 0)
    def _():
        h_ref[...] = _rms(x_ref[...], g_ref[...]).astype(BF16)

    o_ref[...] = jnp.dot(h_ref[...], w_ref[...], preferred_element_type=F32).astype(o_ref.dtype)


def _in_proj(x2, g, w):
    t, d = x2.shape
    n = w.shape[1]
    tm = _pick(t, (1024, 512, 256, 128))
    tn = _pick(n, (1024, 768, 512, 384, 256, 128))
    return pl.pallas_call(
        _in_proj_kernel,
        grid=(t // tm, n // tn),
        in_specs=[
            pl.BlockSpec((tm, d), lambda i, j: (i, 0)),
            pl.BlockSpec((1, d), lambda i, j: (0, 0)),
            pl.BlockSpec((d, tn), lambda i, j: (0, j)),
        ],
        out_specs=pl.BlockSpec((tm, tn), lambda i, j: (i, j)),
        out_shape=jax.ShapeDtypeStruct((t, n), BF16),
        scratch_shapes=[pltpu.VMEM((tm, d), BF16)],
        compiler_params=pltpu.CompilerParams(
            dimension_semantics=("parallel", "arbitrary"), vmem_limit_bytes=VMEM_LIMIT),
        name="in_proj",
    )(x2, g, w)


def _dilated_kernel(q_ref, k_ref, v_ref, bias_ref, o_ref, qf, kf, vf, acc, m_st, l_st, *, configs, pad,
                    group):
    s_len = q_ref.shape[1]
    scale = 1.0 / math.sqrt(HEAD_DIM)

    qf[...] = q_ref[0].astype(F32)
    zeros = jnp.zeros((pad, HEAD_DIM), F32)
    for buf, src in ((kf, k_ref), (vf, v_ref)):
        buf[pl.ds(0, pad), :] = zeros
        buf[pl.ds(pad + s_len, pad), :] = zeros
        buf[pl.ds(pad, s_len), :] = src[0].astype(F32)

    for ci, (window, dil) in enumerate(configs):
        half = window // (2 * dil)
        blk = 2 * half
        nkeys = 4 * half
        sub_len = s_len // dil
        shift = dil.bit_length() - 1

        def rows(start, size, dil=dil):
            if dil == 1:
                return pl.ds(pl.multiple_of(start, 8), size)
            return pl.ds(start, size, stride=dil)

        def block(it, ci=ci, dil=dil, half=half, blk=blk, nkeys=nkeys, sub_len=sub_len,
                  shift=shift, rows=rows):
            phase = it & (dil - 1)
            n = it >> shift
            sub0 = n * blk
            q_rows = rows(phase + dil * sub0, blk)
            kv_rows = rows(pad + phase + dil * (sub0 - half), nkeys)
            qb = qf[q_rows, :].astype(BF16)
            kb = kf[kv_rows, :].astype(BF16)
            vb = vf[kv_rows, :].astype(BF16)
            if ci > 0:
                m_old, l_old, a_old = m_st[q_rows, :], l_st[q_rows, :], acc[q_rows, :]
            s = lax.dot_general(qb, kb, (((1,), (1,)), ((), ())), preferred_element_type=F32)
            key_sub = sub0 - half + lax.broadcasted_iota(jnp.int32, (1, nkeys), 1)
            valid = jnp.where((key_sub >= 0) & (key_sub < sub_len), 0.0, NEG_INF)
            s = s * scale + bias_ref[0, ci] + valid
            m_cur = jnp.max(s, axis=1, keepdims=True)
            if ci == 0:
                m_new = jnp.broadcast_to(m_cur, (blk, LANES))
            else:
                m_new = jnp.maximum(m_old, m_cur)
            p = jnp.exp(s - jnp.concatenate([m_new] * (nkeys // LANES), axis=1))
            l_new = jnp.sum(p, axis=1, keepdims=True)
            a_new = jnp.dot(p.astype(BF16), vb, preferred_element_type=F32)
            if ci == 0:
                l_new = jnp.broadcast_to(l_new, (blk, LANES))
            else:
                alpha = jnp.exp(m_old - m_new)
                l_new = alpha * l_old + l_new
                a_new = alpha * a_old + a_new
            return q_rows, m_new, l_new, a_new

        def body(j, carry, block=block):
            results = [block(j * group + g) for g in range(group)]
            for q_rows, m_new, l_new, a_new in results:
                m_st[q_rows, :] = m_new
                l_st[q_rows, :] = l_new
                acc[q_rows, :] = a_new
            return carry

        lax.fori_loop(0, s_len // (blk * group), body, 0)

    o_ref[0] = (acc[...] / l_st[...]).astype(o_ref.dtype)


def _dilated_bias(rel_bias, n_heads):
    tiles = []
    for window, dil in DILATED_CONFIGS:
        half = window // (2 * dil)
        rel = (jnp.arange(4 * half)[None, :] - half) - jnp.arange(2 * half)[:, None]
        b = _bias_lookup(rel_bias[:, :n_heads], _rel_bucket(rel * dil, rel_bias.shape[0]))
        tiles.append(jnp.where((jnp.abs(rel) <= half)[None], b, NEG_INF))
    return jnp.stack(tiles, axis=1)


def _dilated_attn(proj3, bias, n_heads):
    b, s_len, _ = proj3.shape
    halves = {w // (2 * d) for w, d in DILATED_CONFIGS}
    assert len(halves) == 1, "one bias tile shape for all branches"
    half = halves.pop()
    assert 2 * half == LANES, "state rows are stored one 128-lane slab per query block"
    pad = half * max(d for _, d in DILATED_CONFIGS)
    for _, d in DILATED_CONFIGS:
        assert s_len % (2 * half * d) == 0 and d & (d - 1) == 0
    group = _pick(s_len // (2 * half), (4, 2, 1))
    kern = functools.partial(_dilated_kernel, configs=DILATED_CONFIGS, pad=pad, group=group)
    col = lambda off: (lambda bi, h: (bi, 0, off + h))
    return pl.pallas_call(
        kern,
        grid=(b, n_heads),
        in_specs=[
            pl.BlockSpec((1, s_len, HEAD_DIM), col(0)),
            pl.BlockSpec((1, s_len, HEAD_DIM), col(n_heads)),
            pl.BlockSpec((1, s_len, HEAD_DIM), col(2 * n_heads)),
            pl.BlockSpec((1,) + bias.shape[1:], lambda bi, h: (h, 0, 0, 0)),
        ],
        out_specs=pl.BlockSpec((1, s_len, HEAD_DIM), lambda bi, h: (bi, 0, h)),
        out_shape=jax.ShapeDtypeStruct((b, s_len, n_heads * HEAD_DIM), F32),
        scratch_shapes=[
            pltpu.VMEM((s_len, HEAD_DIM), F32),
            pltpu.VMEM((s_len + 2 * pad, HEAD_DIM), F32),
            pltpu.VMEM((s_len + 2 * pad, HEAD_DIM), F32),
            pltpu.VMEM((s_len, HEAD_DIM), F32),
            pltpu.VMEM((s_len, LANES), F32),
            pltpu.VMEM((s_len, LANES), F32),
        ],
        compiler_params=pltpu.CompilerParams(
            dimension_semantics=("parallel", "parallel"), vmem_limit_bytes=VMEM_LIMIT),
        name="dilated_attn",
    )(proj3, proj3, proj3, bias)


def _diff_attn_kernel(q_ref, k_ref, v_ref, bias_ref, lq1_ref, lk1_ref, lq2_ref, lk2_ref, g_ref,
                      o_ref, qs_ref, s0_ref, s1_ref, m_ref, acc_ref, vx_ref, *, tq, tk, lambda_init):
    s_len = k_ref.shape[1]
    nq, nk = s_len // tq, s_len // tk
    qsub, ksub = tq // LANES, tk // LANES
    dn_t = (((1,), (1,)), ((), ()))
    s_refs = (s0_ref, s1_ref)
    assert nq % 2 == 0

    vx_ref[:, pl.ds(0, HEAD_DIM)] = v_ref[0]
    vx_ref[:, pl.ds(HEAD_DIM, LANES)] = jnp.ones((s_len, LANES), BF16)
    acc_ref[...] = jnp.zeros(acc_ref.shape, F32)
    lam = (jnp.exp(jnp.sum(lq1_ref[...] * lk1_ref[...], axis=-1, keepdims=True))
           - jnp.exp(jnp.sum(lq2_ref[...] * lk2_ref[...], axis=-1, keepdims=True))
           + lambda_init)

    def start_tile(t):
        q = q_ref[0, pl.ds(pl.multiple_of(t * tq, tq), tq), :] * (1.0 / math.sqrt(DIFF_QK_DIM))
        lane = lax.broadcasted_iota(jnp.int32, (tq, HEAD_DIM), 1)
        qs_ref[pl.ds(0, tq), :] = jnp.where(lane < DIFF_QK_DIM, q, 0).astype(BF16)
        qs_ref[pl.ds(tq, tq), :] = jnp.where(lane >= DIFF_QK_DIM, q, 0).astype(BF16)
        m_ref[0] = jnp.full((2 * tq, LANES), NEG_INF, F32)

    def score_chunk(t, c, slot):
        k0 = c * tk
        s = lax.dot_general(qs_ref[...], k_ref[0, pl.ds(k0, tk), :], dn_t, preferred_element_type=F32)
        bias_rows = []
        for a in range(qsub):
            tiles = []
            for cc in range(ksub):
                diff = (c * ksub + cc) - (t * qsub + a)
                idx = jnp.minimum(jnp.maximum(diff, -DIFF_BIAS_DMAX), DIFF_BIAS_DMAX) + DIFF_BIAS_DMAX
                tiles.append(bias_ref[0, idx])
            bias_rows.append(jnp.concatenate(tiles, axis=1))
        bias = jnp.concatenate(bias_rows, axis=0)
        s = s + jnp.concatenate([bias, bias], axis=0)
        s_refs[slot][:, pl.ds(k0, tk)] = s
        m_run = m_ref[0]
        for cc in range(ksub):
            m_run = jnp.maximum(m_run, s[:, cc * LANES:(cc + 1) * LANES])
        m_ref[0] = m_run

    def finish_scores():
        m_ref[1] = jnp.broadcast_to(jnp.max(m_ref[0], axis=1, keepdims=True), (2 * tq, LANES))

    def value_chunk(c, slot):
        k0 = c * tk
        m_b = jnp.concatenate([m_ref[1]] * ksub, axis=1)
        p = jnp.exp(s_refs[slot][:, pl.ds(k0, tk)] - m_b).astype(BF16)
        acc_ref[...] += jnp.dot(p, vx_ref[pl.ds(k0, tk), :], preferred_element_type=F32)

    def finish_tile(t):
        acc = acc_ref[...]
        o = acc[:, :HEAD_DIM] / acc[:, HEAD_DIM:]
        od = o[:tq] - lam * o[tq:]
        y = _rms(od, g_ref[...]) * (1.0 - lambda_init)
        o_ref[0, pl.ds(pl.multiple_of(t * tq, tq), tq), :] = y.astype(o_ref.dtype)
        acc_ref[...] = jnp.zeros(acc_ref.shape, F32)

    def tile_step(t, slot):
        start_tile(t)
        for c in range(nk):
            value_chunk(c, 1 - slot)
            score_chunk(t, c, slot)
        finish_tile(t - 1)
        finish_scores()

    start_tile(0)
    for c in range(nk):
        score_chunk(0, c, 0)
    finish_scores()

    def pair_step(u, carry):
        tile_step(2 * u + 1, 1)
        tile_step(2 * u + 2, 0)
        return carry

    lax.fori_loop(0, nq // 2 - 1, pair_step, 0)
    tile_step(nq - 1, 1)
    for c in range(nk):
        value_chunk(c, 1)
    finish_tile(nq - 1)


def _diff_bias(rel_bias, n_heads_a, n_heads_b):
    t = jnp.arange(2 * DIFF_BIAS_DMAX + 1) - DIFF_BIAS_DMAX
    rel = (LANES * t[:, None, None] + jnp.arange(LANES)[None, None, :] - jnp.arange(LANES)[None, :, None])
    return _bias_lookup(rel_bias[:, n_heads_a:n_heads_a + n_heads_b], _rel_bucket(rel, rel_bias.shape[0]))


def _diff_attn(proj3, bias, lams, subln_g, n_heads_a, n_heads_b, lambda_init):
    b, s_len, _ = proj3.shape
    tq = _pick(s_len, (256, 128))
    tk = _pick(s_len, (512, 256, 128))
    base = 3 * n_heads_a
    kern = functools.partial(_diff_attn_kernel, tq=tq, tk=tk, lambda_init=lambda_init)
    vec = lambda n: pl.BlockSpec((1, n), lambda bi, h: (0, 0))
    col = lambda off: (lambda bi, h: (bi, 0, off + h))
    return pl.pallas_call(
        kern,
        grid=(b, n_heads_b),
        in_specs=[
            pl.BlockSpec((1, s_len, HEAD_DIM), col(base)),
            pl.BlockSpec((1, s_len, HEAD_DIM), col(base + n_heads_b)),
            pl.BlockSpec((1, s_len, HEAD_DIM), col(base + 2 * n_heads_b)),
            pl.BlockSpec((1,) + bias.shape[1:], lambda bi, h: (h, 0, 0, 0)),
            vec(DIFF_QK_DIM), vec(DIFF_QK_DIM), vec(DIFF_QK_DIM), vec(DIFF_QK_DIM),
            vec(HEAD_DIM),
        ],
        out_specs=pl.BlockSpec((1, s_len, HEAD_DIM), col(0)),
        out_shape=jax.ShapeDtypeStruct((b, s_len, n_heads_b * HEAD_DIM), BF16),
        scratch_shapes=[
            pltpu.VMEM((2 * tq, HEAD_DIM), BF16),
            pltpu.VMEM((2 * tq, s_len), F32),
            pltpu.VMEM((2 * tq, s_len), F32),
            pltpu.VMEM((2, 2 * tq, LANES), F32),
            pltpu.VMEM((2 * tq, HEAD_DIM + LANES), F32),
            pltpu.VMEM((s_len, HEAD_DIM + LANES), BF16),
        ],
        compiler_params=pltpu.CompilerParams(
            dimension_semantics=("parallel", "parallel"), vmem_limit_bytes=VMEM_LIMIT),
        name="diff_attn",
    )(proj3, proj3, proj3, bias, *lams, subln_g)


def _out_proj_kernel(oa_ref, ob_ref, x_ref, wa_ref, wb_ref, ga_ref, gp_ref, o_ref):
    oa = _rms(oa_ref[...], ga_ref[...]).astype(BF16)
    y = jnp.dot(oa, wa_ref[...], preferred_element_type=F32)
    y = y + jnp.dot(ob_ref[...], wb_ref[...], preferred_element_type=F32)
    o_ref[...] = x_ref[...] + _rms(y, gp_ref[...])


def _out_proj(oa2, ob2, x2, wa, wb, ga, gp):
    t, d = x2.shape
    wa_w, wb_w = wa.shape[0], wb.shape[0]
    tm = _pick(t, (512, 256, 128))
    const = lambda i: (0, 0)
    row = lambda i: (i, 0)
    return pl.pallas_call(
        _out_proj_kernel,
        grid=(t // tm,),
        in_specs=[
            pl.BlockSpec((tm, wa_w), row),
            pl.BlockSpec((tm, wb_w), row),
            pl.BlockSpec((tm, d), row),
            pl.BlockSpec((wa_w, d), const),
            pl.BlockSpec((wb_w, d), const),
            pl.BlockSpec((1, wa_w), const),
            pl.BlockSpec((1, d), const),
        ],
        out_specs=pl.BlockSpec((tm, d), row),
        out_shape=jax.ShapeDtypeStruct((t, d), F32),
        compiler_params=pltpu.CompilerParams(
            dimension_semantics=("parallel",), vmem_limit_bytes=VMEM_LIMIT),
        name="out_proj",
    )(oa2, ob2, x2, wa, wb, ga, gp)


def _ffn_kernel(x_ref, gpre_ref, wg_ref, wu_ref, wd_ref, gpost_ref, o_ref, h_ref, acc_ref):
    j = pl.program_id(1)

    @pl.when(j == 0)
    def _():
        h_ref[...] = _rms(x_ref[...], gpre_ref[...]).astype(BF16)
        acc_ref[...] = jnp.zeros(acc_ref.shape, F32)

    h = h_ref[...]
    gate = jnp.dot(h, wg_ref[...], preferred_element_type=F32)
    up = jnp.dot(h, wu_ref[...], preferred_element_type=F32)
    f = gate * (1.0 / (1.0 + jnp.exp(-gate))) * up
    acc_ref[...] += jnp.dot(f.astype(BF16), wd_ref[...], preferred_element_type=F32)

    @pl.when(j == pl.num_programs(1) - 1)
    def _():
        o_ref[...] = x_ref[...] + _rms(acc_ref[...], gpost_ref[...])


def _ffn(x2, gpre, wg, wu, wd, gpost):
    t, d = x2.shape
    f = wg.shape[1]
    tm = _pick(t, (512, 256, 128))
    tf = _pick(f, (512, 256, 128))
    return pl.pallas_call(
        _ffn_kernel,
        grid=(t // tm, f // tf),
        in_specs=[
            pl.BlockSpec((tm, d), lambda i, j: (i, 0)),
            pl.BlockSpec((1, d), lambda i, j: (0, 0)),
            pl.BlockSpec((d, tf), lambda i, j: (0, j)),
            pl.BlockSpec((d, tf), lambda i, j: (0, j)),
            pl.BlockSpec((tf, d), lambda i, j: (j, 0)),
            pl.BlockSpec((1, d), lambda i, j: (0, 0)),
        ],
        out_specs=pl.BlockSpec((tm, d), lambda i, j: (i, 0)),
        out_shape=jax.ShapeDtypeStruct((t, d), F32),
        scratch_shapes=[pltpu.VMEM((tm, d), BF16), pltpu.VMEM((tm, d), F32)],
        compiler_params=pltpu.CompilerParams(
            dimension_semantics=("parallel", "arbitrary"), vmem_limit_bytes=VMEM_LIMIT),
        name="ffn",
    )(x2, gpre, wg, wu, wd, gpost)


def _encoder_layer(x, layer, p, bias_a, bias_b):
    b, s_len, d = x.shape
    width = p["w_in"].shape[1] // 6
    n_heads = width // HEAD_DIM
    lambda_init = 0.8 - 0.6 * math.exp(-0.3 * layer)
    row = lambda v: v.reshape(1, -1)

    x2 = x.reshape(b * s_len, d)
    proj3 = _in_proj(x2, row(p["pre_mix_g"]), p["w_in"]).reshape(b, s_len, 6 * width)
    oa = _dilated_attn(proj3, bias_a, n_heads)
    lams = [row(p[k]) for k in ("lambda_q1", "lambda_k1", "lambda_q2", "lambda_k2")]
    ob = _diff_attn(proj3, bias_b, lams, row(p["subln_g"]), n_heads, n_heads, lambda_init)
    x1 = _out_proj(oa.reshape(b * s_len, width), ob.reshape(b * s_len, width), x2,
                   p["w_out"][:width], p["w_out"][width:], row(p["norm_a_g"]), row(p["post_mix_g"]))
    y = _ffn(x1, row(p["pre_ffn_g"]), p["w_gate"], p["w_up"], p["w_down"], row(p["post_ffn_g"]))
    return y.reshape(b, s_len, d)


def kernel(x_prompt, x_sample, pre_mix_g, post_mix_g, pre_ffn_g, post_ffn_g, w_in, w_out, norm_a_g,
           lambda_q1, lambda_k1, lambda_q2, lambda_k2, subln_g, w_gate, w_up, w_down, rel_bias):
    depth = w_in.shape[0]
    n_heads = w_in.shape[2] // 6 // HEAD_DIM
    bias_a = _dilated_bias(rel_bias, n_heads)
    bias_b = _diff_bias(rel_bias, n_heads, n_heads)
    layers = []
    for layer in range(depth):
        layers.append(dict(
            pre_mix_g=pre_mix_g[layer], post_mix_g=post_mix_g[layer], pre_ffn_g=pre_ffn_g[layer],
            post_ffn_g=post_ffn_g[layer], norm_a_g=norm_a_g[layer], subln_g=subln_g[layer],
            lambda_q1=lambda_q1[layer], lambda_k1=lambda_k1[layer],
            lambda_q2=lambda_q2[layer], lambda_k2=lambda_k2[layer],
            w_in=w_in[layer].astype(BF16), w_out=w_out[layer].astype(BF16),
            w_gate=w_gate[layer].astype(BF16), w_up=w_up[layer].astype(BF16),
            w_down=w_down[layer].astype(BF16)))
    outs = []
    for x in (x_prompt, x_sample):
        for layer, p in enumerate(layers):
            x = _encoder_layer(x, layer, p, bias_a, bias_b)
        outs.append(x)
    return tuple(outs)
```

```python
import functools
import math

import jax
import jax.numpy as jnp
from jax import lax
from jax.experimental import pallas as pl
from jax.experimental.pallas import tpu as pltpu

F32 = jnp.float32
BF16 = jnp.bfloat16

EPS = 1e-6
NEG_INF = -1e30
HEAD_DIM = 128
DIFF_QK_DIM = 64
DILATED_CONFIGS = ((128, 1), (512, 4), (2048, 16))
MAX_DISTANCE = 1024
LANES = 128
VMEM_LIMIT = 56 * 1024 * 1024

DIFF_BIAS_DMAX = -(-(MAX_DISTANCE + LANES - 1) // LANES)


def _pick(n, candidates):
    for c in candidates:
        if n % c == 0:
            return c
    raise ValueError(f"no tile in {candidates} divides {n}")


def _rms(x, g):
    ms = jnp.mean(x * x, axis=-1, keepdims=True)
    return x * lax.rsqrt(ms + EPS) * g


def _rel_bucket(rel, n_buckets):
    half = n_buckets // 2
    max_exact = half // 2
    n = jnp.abs(rel)
    nf = jnp.maximum(n, 1).astype(F32)
    large = max_exact + (jnp.log(nf / max_exact) / math.log(MAX_DISTANCE / max_exact)
                         * (half - max_exact)).astype(jnp.int32)
    large = jnp.minimum(large, half - 1)
    return jnp.where(rel > 0, half, 0) + jnp.where(n < max_exact, n, large)


def _bias_lookup(table, bucket):
    cols = table.astype(F32).T.reshape((table.shape[1],) + (1,) * bucket.ndim + (table.shape[0],))
    out = jnp.zeros((table.shape[1],) + bucket.shape, F32)
    for b in range(table.shape[0]):
        out = jnp.where(bucket[None] == b, cols[..., b], out)
    return out


def _in_proj_kernel(x_ref, g_ref, w_ref, o_ref, h_ref):
    @pl.when(pl.program_id(1) == 0)
    def _():
        h_ref[...] = _rms(x_ref[...], g_ref[...]).astype(BF16)

    o_ref[...] = jnp.dot(h_ref[...], w_ref[...], preferred_element_type=F32).astype(o_ref.dtype)


def _in_proj(x2, g, w):
    t, d = x2.shape
    n = w.shape[1]
    tm = _pick(t, (1024, 512, 256, 128))
    tn = _pick(n, (1024, 768, 512, 384, 256, 128))
    return pl.pallas_call(
        _in_proj_kernel,
        grid=(t // tm, n // tn),
        in_specs=[
            pl.BlockSpec((tm, d), lambda i, j: (i, 0)),
            pl.BlockSpec((1, d), lambda i, j: (0, 0)),
            pl.BlockSpec((d, tn), lambda i, j: (0, j)),
        ],
        out_specs=pl.BlockSpec((tm, tn), lambda i, j: (i, j)),
        out_shape=jax.ShapeDtypeStruct((t, n), BF16),
        scratch_shapes=[pltpu.VMEM((tm, d), BF16)],
        compiler_params=pltpu.CompilerParams(
            dimension_semantics=("parallel", "arbitrary"), vmem_limit_bytes=VMEM_LIMIT),
        name="in_proj",
    )(x2, g, w)


def _dilated_plan(s_len):
    plans = []
    for window, dil in sorted(DILATED_CONFIGS, key=lambda c: -c[1]):
        half = window // (2 * dil)
        sub_len = s_len // dil
        assert s_len % dil == 0 and dil & (dil - 1) == 0
        if sub_len <= 2 * LANES:
            blk, nkeys, nblocks, whole = sub_len, sub_len, dil, True
        else:
            blk, nkeys, nblocks, whole = 2 * half, 4 * half, s_len // (2 * half), False
            assert sub_len % blk == 0 and sub_len // blk >= 2
        assert blk % 8 == 0 and nkeys % LANES == 0
        group = _pick(nblocks, [g for g in (8, 4, 2, 1) if g * blk * nkeys <= 8 * LANES * 2 * LANES])
        plans.append(dict(dil=dil, half=half, sub_len=sub_len, blk=blk, nkeys=nkeys, nblocks=nblocks,
                          whole=whole, group=group))
    return plans


def _dilated_kernel(q_ref, k_ref, v_ref, *rest, plans, pad, pad16):
    bias_refs = rest[:len(plans)]
    o_ref, qf, kf, vf, k16, v16, acc, m_st, l_st = rest[len(plans):]
    s_len = q_ref.shape[1]
    scale = 1.0 / math.sqrt(HEAD_DIM)

    qf[...] = q_ref[0].astype(F32)
    for buf, src in ((kf, k_ref), (vf, v_ref)):
        if pad:
            buf[pl.ds(0, pad), :] = jnp.zeros((pad, HEAD_DIM), F32)
            buf[pl.ds(pad + s_len, pad), :] = jnp.zeros((pad, HEAD_DIM), F32)
        buf[pl.ds(pad, s_len), :] = src[0].astype(F32)
    for buf, src in ((k16, k_ref), (v16, v_ref)):
        buf[pl.ds(0, pad16), :] = jnp.zeros((pad16, HEAD_DIM), BF16)
        buf[pl.ds(pad16 + s_len, pad16), :] = jnp.zeros((pad16, HEAD_DIM), BF16)
        buf[pl.ds(pad16, s_len), :] = src[0]

    for ci, plan in enumerate(plans):
        dil, half, blk, nkeys = plan["dil"], plan["half"], plan["blk"], plan["nkeys"]
        whole, group = plan["whole"], plan["group"]
        shift = dil.bit_length() - 1
        nblk_phase = plan["sub_len"] // blk
        bias_ref = bias_refs[ci]

        def scores(it, ci=ci, dil=dil, half=half, blk=blk, nkeys=nkeys, whole=whole, shift=shift,
                   nblk_phase=nblk_phase, bias_ref=bias_ref):
            if whole:
                phase, sub0, key0, variant = it, 0, 0, 0
            else:
                phase, n = it & (dil - 1), it >> shift
                sub0 = n * blk
                key0 = sub0 - half
                variant = jnp.where(n == 0, 1, jnp.where(n == nblk_phase - 1, 2, 0))
            if dil == 1:
                q_rows = pl.ds(pl.multiple_of(sub0, blk), blk)
                kv_rows = pl.ds(pl.multiple_of(pad16 + key0, half), nkeys)
                qb, kb, vb = q_ref[0, q_rows, :], k16[kv_rows, :], v16[kv_rows, :]
            else:
                q_rows = pl.ds(phase + dil * sub0, blk, stride=dil)
                kv_rows = pl.ds(pad + phase + dil * key0, nkeys, stride=dil)
                qb, kb, vb = (qf[q_rows, :].astype(BF16), kf[kv_rows, :].astype(BF16),
                              vf[kv_rows, :].astype(BF16))
            old = (m_st[q_rows, :], l_st[q_rows, :], acc[q_rows, :]) if ci > 0 else None
            s = lax.dot_general(qb, kb, (((1,), (1,)), ((), ())), preferred_element_type=F32)
            return q_rows, vb, old, s * scale + bias_ref[0, variant]

        def softmax(old, s, ci=ci, blk=blk, nkeys=nkeys):
            m_cur = jnp.max(s, axis=1, keepdims=True)
            if ci == 0:
                m_new = jnp.broadcast_to(m_cur, (blk, LANES))
            else:
                m_new = jnp.maximum(old[0], m_cur)
            p = jnp.exp(s - jnp.concatenate([m_new] * (nkeys // LANES), axis=1))
            return m_new, jnp.sum(p, axis=1, keepdims=True), p.astype(BF16)

        def values(old, vb, m_new, l_new, p, ci=ci, blk=blk):
            a_new = jnp.dot(p, vb, preferred_element_type=F32)
            if ci == 0:
                return jnp.broadcast_to(l_new, (blk, LANES)), a_new
            alpha = jnp.exp(old[0] - m_new)
            return alpha * old[1] + l_new, alpha * old[2] + a_new

        def body(j, carry, scores=scores, softmax=softmax, values=values, group=group):
            staged = [scores(j * group + g) for g in range(group)]
            probs = [softmax(old, s) for _, _, old, s in staged]
            for (q_rows, vb, old, _), (m_new, l_new, p) in zip(staged, probs):
                l_new, a_new = values(old, vb, m_new, l_new, p)
                m_st[q_rows, :] = m_new
                l_st[q_rows, :] = l_new
                acc[q_rows, :] = a_new
            return carry

        lax.fori_loop(0, plan["nblocks"] // group, body, 0)

    o_ref[0] = (acc[...] / l_st[...]).astype(o_ref.dtype)


def _dilated_bias(rel_bias, n_heads, plans):
    tables = []
    for plan in plans:
        half, blk, nkeys = plan["half"], plan["blk"], plan["nkeys"]
        key0 = 0 if plan["whole"] else -half
        col = jnp.arange(nkeys)[None, :]
        rel = (col + key0) - jnp.arange(blk)[:, None]
        b = _bias_lookup(rel_bias[:, :n_heads], _rel_bucket(rel * plan["dil"], rel_bias.shape[0]))
        band = jnp.abs(rel) <= half
        if plan["whole"]:
            masks = [band]
        else:
            masks = [band, band & (col >= half), band & (col < blk + half)]
        tables.append(jnp.stack([jnp.where(mk[None], b, NEG_INF) for mk in masks], axis=1))
    return tables


def _dilated_attn(proj3, biases, n_heads, plans):
    b, s_len, _ = proj3.shape
    pad = max([p["half"] * p["dil"] for p in plans if not p["whole"] and p["dil"] > 1], default=0)
    pad16 = max([p["half"] for p in plans if p["dil"] == 1], default=16)
    pad16 = -(-pad16 // 16) * 16
    kern = functools.partial(_dilated_kernel, plans=plans, pad=pad, pad16=pad16)
    col = lambda off: (lambda bi, h: (bi, 0, off + h))
    return pl.pallas_call(
        kern,
        grid=(b, n_heads),
        in_specs=[
            pl.BlockSpec((1, s_len, HEAD_DIM), col(0)),
            pl.BlockSpec((1, s_len, HEAD_DIM), col(n_heads)),
            pl.BlockSpec((1, s_len, HEAD_DIM), col(2 * n_heads)),
        ] + [pl.BlockSpec((1,) + t.shape[1:], lambda bi, h: (h, 0, 0, 0)) for t in biases],
        out_specs=pl.BlockSpec((1, s_len, HEAD_DIM), lambda bi, h: (bi, 0, h)),
        out_shape=jax.ShapeDtypeStruct((b, s_len, n_heads * HEAD_DIM), F32),
        scratch_shapes=[
            pltpu.VMEM((s_len, HEAD_DIM), F32),
            pltpu.VMEM((s_len + 2 * pad, HEAD_DIM), F32),
            pltpu.VMEM((s_len + 2 * pad, HEAD_DIM), F32),
            pltpu.VMEM((s_len + 2 * pad16, HEAD_DIM), BF16),
            pltpu.VMEM((s_len + 2 * pad16, HEAD_DIM), BF16),
            pltpu.VMEM((s_len, HEAD_DIM), F32),
            pltpu.VMEM((s_len, LANES), F32),
            pltpu.VMEM((s_len, LANES), F32),
        ],
        compiler_params=pltpu.CompilerParams(
            dimension_semantics=("parallel", "parallel"), vmem_limit_bytes=VMEM_LIMIT),
        name="dilated_attn",
    )(proj3, proj3, proj3, *biases)


def _diff_attn_kernel(q_ref, k_ref, v_ref, bias_ref, lq1_ref, lk1_ref, lq2_ref, lk2_ref, g_ref,
                      o_ref, qs_ref, s0_ref, s1_ref, m_ref, acc_ref, vx_ref, *, tq, tk, lambda_init):
    s_len = k_ref.shape[1]
    nq, nk = s_len // tq, s_len // tk
    qsub, ksub = tq // LANES, tk // LANES
    dn_t = (((1,), (1,)), ((), ()))
    s_refs = (s0_ref, s1_ref)
    assert nq % 2 == 0

    vx_ref[:, pl.ds(0, HEAD_DIM)] = v_ref[0]
    vx_ref[:, pl.ds(HEAD_DIM, LANES)] = jnp.ones((s_len, LANES), BF16)
    acc_ref[...] = jnp.zeros(acc_ref.shape, F32)
    lam = (jnp.exp(jnp.sum(lq1_ref[...] * lk1_ref[...], axis=-1, keepdims=True))
           - jnp.exp(jnp.sum(lq2_ref[...] * lk2_ref[...], axis=-1, keepdims=True))
           + lambda_init)

    def start_tile(t):
        q = q_ref[0, pl.ds(pl.multiple_of(t * tq, tq), tq), :] * (1.0 / math.sqrt(DIFF_QK_DIM))
        lane = lax.broadcasted_iota(jnp.int32, (tq, HEAD_DIM), 1)
        qs_ref[pl.ds(0, tq), :] = jnp.where(lane < DIFF_QK_DIM, q, 0).astype(BF16)
        qs_ref[pl.ds(tq, tq), :] = jnp.where(lane >= DIFF_QK_DIM, q, 0).astype(BF16)
        m_ref[0] = jnp.full((2 * tq, LANES), NEG_INF, F32)

    def score_chunk(t, c, slot):
        k0 = c * tk
        s = lax.dot_general(qs_ref[...], k_ref[0, pl.ds(k0, tk), :], dn_t, preferred_element_type=F32)
        bias_rows = []
        for a in range(qsub):
            tiles = []
            for cc in range(ksub):
                diff = (c * ksub + cc) - (t * qsub + a)
                idx = jnp.minimum(jnp.maximum(diff, -DIFF_BIAS_DMAX), DIFF_BIAS_DMAX) + DIFF_BIAS_DMAX
                tiles.append(bias_ref[0, idx])
            bias_rows.append(jnp.concatenate(tiles, axis=1))
        bias = jnp.concatenate(bias_rows, axis=0)
        s = s + jnp.concatenate([bias, bias], axis=0)
        s_refs[slot][:, pl.ds(k0, tk)] = s
        m_run = m_ref[0]
        for cc in range(ksub):
            m_run = jnp.maximum(m_run, s[:, cc * LANES:(cc + 1) * LANES])
        m_ref[0] = m_run

    def finish_scores():
        m_ref[1] = jnp.broadcast_to(jnp.max(m_ref[0], axis=1, keepdims=True), (2 * tq, LANES))

    def value_chunk(c, slot):
        k0 = c * tk
        m_b = jnp.concatenate([m_ref[1]] * ksub, axis=1)
        p = jnp.exp(s_refs[slot][:, pl.ds(k0, tk)] - m_b).astype(BF16)
        acc_ref[...] += jnp.dot(p, vx_ref[pl.ds(k0, tk), :], preferred_element_type=F32)

    def finish_tile(t):
        acc = acc_ref[...]
        o = acc[:, :HEAD_DIM] / acc[:, HEAD_DIM:]
        od = o[:tq] - lam * o[tq:]
        y = _rms(od, g_ref[...]) * (1.0 - lambda_init)
        o_ref[0, pl.ds(pl.multiple_of(t * tq, tq), tq), :] = y.astype(o_ref.dtype)
        acc_ref[...] = jnp.zeros(acc_ref.shape, F32)

    def tile_step(t, slot):
        start_tile(t)
        for c in range(nk):
            value_chunk(c, 1 - slot)
            score_chunk(t, c, slot)
        finish_tile(t - 1)
        finish_scores()

    start_tile(0)
    for c in range(nk):
        score_chunk(0, c, 0)
    finish_scores()

    def pair_step(u, carry):
        tile_step(2 * u + 1, 1)
        tile_step(2 * u + 2, 0)
        return carry

    lax.fori_loop(0, nq // 2 - 1, pair_step, 0)
    tile_step(nq - 1, 1)
    for c in range(nk):
        value_chunk(c, 1)
    finish_tile(nq - 1)


def _diff_bias(rel_bias, n_heads_a, n_heads_b):
    t = jnp.arange(2 * DIFF_BIAS_DMAX + 1) - DIFF_BIAS_DMAX
    rel = (LANES * t[:, None, None] + jnp.arange(LANES)[None, None, :] - jnp.arange(LANES)[None, :, None])
    return _bias_lookup(rel_bias[:, n_heads_a:n_heads_a + n_heads_b], _rel_bucket(rel, rel_bias.shape[0]))


def _diff_attn(proj3, bias, lams, subln_g, n_heads_a, n_heads_b, lambda_init):
    b, s_len, _ = proj3.shape
    tq = _pick(s_len, (256, 128))
    tk = _pick(s_len, (512, 256, 128))
    base = 3 * n_heads_a
    kern = functools.partial(_diff_attn_kernel, tq=tq, tk=tk, lambda_init=lambda_init)
    vec = lambda n: pl.BlockSpec((1, n), lambda bi, h: (0, 0))
    col = lambda off: (lambda bi, h: (bi, 0, off + h))
    return pl.pallas_call(
        kern,
        grid=(b, n_heads_b),
        in_specs=[
            pl.BlockSpec((1, s_len, HEAD_DIM), col(base)),
            pl.BlockSpec((1, s_len, HEAD_DIM), col(base + n_heads_b)),
            pl.BlockSpec((1, s_len, HEAD_DIM), col(base + 2 * n_heads_b)),
            pl.BlockSpec((1,) + bias.shape[1:], lambda bi, h: (h, 0, 0, 0)),
            vec(DIFF_QK_DIM), vec(DIFF_QK_DIM), vec(DIFF_QK_DIM), vec(DIFF_QK_DIM),
            vec(HEAD_DIM),
        ],
        out_specs=pl.BlockSpec((1, s_len, HEAD_DIM), col(0)),
        out_shape=jax.ShapeDtypeStruct((b, s_len, n_heads_b * HEAD_DIM), BF16),
        scratch_shapes=[
            pltpu.VMEM((2 * tq, HEAD_DIM), BF16),
            pltpu.VMEM((2 * tq, s_len), F32),
            pltpu.VMEM((2 * tq, s_len), F32),
            pltpu.VMEM((2, 2 * tq, LANES), F32),
            pltpu.VMEM((2 * tq, HEAD_DIM + LANES), F32),
            pltpu.VMEM((s_len, HEAD_DIM + LANES), BF16),
        ],
        compiler_params=pltpu.CompilerParams(
            dimension_semantics=("parallel", "parallel"), vmem_limit_bytes=VMEM_LIMIT),
        name="diff_attn",
    )(proj3, proj3, proj3, bias, *lams, subln_g)


def _out_proj_kernel(oa_ref, ob_ref, x_ref, wa_ref, wb_ref, ga_ref, gp_ref, o_ref):
    oa = _rms(oa_ref[...], ga_ref[...]).astype(BF16)
    y = jnp.dot(oa, wa_ref[...], preferred_element_type=F32)
    y = y + jnp.dot(ob_ref[...], wb_ref[...], preferred_element_type=F32)
    o_ref[...] = x_ref[...] + _rms(y, gp_ref[...])


def _out_proj(oa2, ob2, x2, wa, wb, ga, gp):
    t, d = x2.shape
    wa_w, wb_w = wa.shape[0], wb.shape[0]
    tm = _pick(t, (512, 256, 128))
    const = lambda i: (0, 0)
    row = lambda i: (i, 0)
    return pl.pallas_call(
        _out_proj_kernel,
        grid=(t // tm,),
        in_specs=[
            pl.BlockSpec((tm, wa_w), row),
            pl.BlockSpec((tm, wb_w), row),
            pl.BlockSpec((tm, d), row),
            pl.BlockSpec((wa_w, d), const),
            pl.BlockSpec((wb_w, d), const),
            pl.BlockSpec((1, wa_w), const),
            pl.BlockSpec((1, d), const),
        ],
        out_specs=pl.BlockSpec((tm, d), row),
        out_shape=jax.ShapeDtypeStruct((t, d), F32),
        compiler_params=pltpu.CompilerParams(
            dimension_semantics=("parallel",), vmem_limit_bytes=VMEM_LIMIT),
        name="out_proj",
    )(oa2, ob2, x2, wa, wb, ga, gp)


def _ffn_kernel(x_ref, gpre_ref, wg_ref, wu_ref, wd_ref, gpost_ref, o_ref, h_ref, acc_ref):
    j = pl.program_id(1)

    @pl.when(j == 0)
    def _():
        h_ref[...] = _rms(x_ref[...], gpre_ref[...]).astype(BF16)
        acc_ref[...] = jnp.zeros(acc_ref.shape, F32)

    h = h_ref[...]
    gate = jnp.dot(h, wg_ref[...], preferred_element_type=F32)
    up = jnp.dot(h, wu_ref[...], preferred_element_type=F32)
    f = gate * (1.0 / (1.0 + jnp.exp(-gate))) * up
    acc_ref[...] += jnp.dot(f.astype(BF16), wd_ref[...], preferred_element_type=F32)

    @pl.when(j == pl.num_programs(1) - 1)
    def _():
        o_ref[...] = x_ref[...] + _rms(acc_ref[...], gpost_ref[...])


def _ffn(x2, gpre, wg, wu, wd, gpost):
    t, d = x2.shape
    f = wg.shape[1]
    tm = _pick(t, (512, 256, 128))
    tf = _pick(f, (512, 256, 128))
    return pl.pallas_call(
        _ffn_kernel,
        grid=(t // tm, f // tf),
        in_specs=[
            pl.BlockSpec((tm, d), lambda i, j: (i, 0)),
            pl.BlockSpec((1, d), lambda i, j: (0, 0)),
            pl.BlockSpec((d, tf), lambda i, j: (0, j)),
            pl.BlockSpec((d, tf), lambda i, j: (0, j)),
            pl.BlockSpec((tf, d), lambda i, j: (j, 0)),
            pl.BlockSpec((1, d), lambda i, j: (0, 0)),
        ],
        out_specs=pl.BlockSpec((tm, d), lambda i, j: (i, 0)),
        out_shape=jax.ShapeDtypeStruct((t, d), F32),
        scratch_shapes=[pltpu.VMEM((tm, d), BF16), pltpu.VMEM((tm, d), F32)],
        compiler_params=pltpu.CompilerParams(
            dimension_semantics=("parallel", "arbitrary"), vmem_limit_bytes=VMEM_LIMIT),
        name="ffn",
    )(x2, gpre, wg, wu, wd, gpost)


def _encoder_layer(x, layer, p, bias_a, bias_b):
    b, s_len, d = x.shape
    width = p["w_in"].shape[1] // 6
    n_heads = width // HEAD_DIM
    lambda_init = 0.8 - 0.6 * math.exp(-0.3 * layer)
    row = lambda v: v.reshape(1, -1)

    x2 = x.reshape(b * s_len, d)
    proj3 = _in_proj(x2, row(p["pre_mix_g"]), p["w_in"]).reshape(b, s_len, 6 * width)
    oa = _dilated_attn(proj3, bias_a[s_len], n_heads, _dilated_plan(s_len))
    lams = [row(p[k]) for k in ("lambda_q1", "lambda_k1", "lambda_q2", "lambda_k2")]
    ob = _diff_attn(proj3, bias_b, lams, row(p["subln_g"]), n_heads, n_heads, lambda_init)
    x1 = _out_proj(oa.reshape(b * s_len, width), ob.reshape(b * s_len, width), x2,
                   p["w_out"][:width], p["w_out"][width:], row(p["norm_a_g"]), row(p["post_mix_g"]))
    y = _ffn(x1, row(p["pre_ffn_g"]), p["w_gate"], p["w_up"], p["w_down"], row(p["post_ffn_g"]))
    return y.reshape(b, s_len, d)


def kernel(x_prompt, x_sample, pre_mix_g, post_mix_g, pre_ffn_g, post_ffn_g, w_in, w_out, norm_a_g,
           lambda_q1, lambda_k1, lambda_q2, lambda_k2, subln_g, w_gate, w_up, w_down, rel_bias):
    depth = w_in.shape[0]
    n_heads = w_in.shape[2] // 6 // HEAD_DIM
    bias_a = {x.shape[1]: _dilated_bias(rel_bias, n_heads, _dilated_plan(x.shape[1]))
              for x in (x_prompt, x_sample)}
    bias_b = _diff_bias(rel_bias, n_heads, n_heads)
    layers = []
    for layer in range(depth):
        layers.append(dict(
            pre_mix_g=pre_mix_g[layer], post_mix_g=post_mix_g[layer], pre_ffn_g=pre_ffn_g[layer],
            post_ffn_g=post_ffn_g[layer], norm_a_g=norm_a_g[layer], subln_g=subln_g[layer],
            lambda_q1=lambda_q1[layer], lambda_k1=lambda_k1[layer],
            lambda_q2=lambda_q2[layer], lambda_k2=lambda_k2[layer],
            w_in=w_in[layer].astype(BF16), w_out=w_out[layer].astype(BF16),
            w_gate=w_gate[layer].astype(BF16), w_up=w_up[layer].astype(BF16),
            w_down=w_down[layer].astype(BF16)))
    outs = []
    for x in (x_prompt, x_sample):
        for layer, p in enumerate(layers):
            x = _encoder_layer(x, layer, p, bias_a, bias_b)
        outs.append(x)
    return tuple(outs)
```

```python
import functools
import math

import jax
import jax.numpy as jnp
from jax import lax
from jax.experimental import pallas as pl
from jax.experimental.pallas import tpu as pltpu

F32 = jnp.float32
BF16 = jnp.bfloat16

EPS = 1e-6
NEG_INF = -1e30
HEAD_DIM = 128
DIFF_QK_DIM = 64
DILATED_CONFIGS = ((128, 1), (512, 4), (2048, 16))
MAX_DISTANCE = 1024
LANES = 128
VMEM_LIMIT = 56 * 1024 * 1024

DIFF_BIAS_DMAX = -(-(MAX_DISTANCE + LANES - 1) // LANES)
LOG2_E = math.log2(math.e)
DIFF_Q_SCALE = LOG2_E / math.sqrt(DIFF_QK_DIM)
ONES_ROWS = 16
DIFF_HEAD_ROWS = 8192


def _pick(n, candidates):
    for c in candidates:
        if n % c == 0:
            return c
    raise ValueError(f"no tile in {candidates} divides {n}")


def _rms(x, g):
    ms = jnp.mean(x * x, axis=-1, keepdims=True)
    return x * lax.rsqrt(ms + EPS) * g


def _rel_bucket(rel, n_buckets):
    half = n_buckets // 2
    max_exact = half // 2
    n = jnp.abs(rel)
    nf = jnp.maximum(n, 1).astype(F32)
    large = max_exact + (jnp.log(nf / max_exact) / math.log(MAX_DISTANCE / max_exact)
                         * (half - max_exact)).astype(jnp.int32)
    large = jnp.minimum(large, half - 1)
    return jnp.where(rel > 0, half, 0) + jnp.where(n < max_exact, n, large)


def _bias_lookup(table, bucket):
    cols = table.astype(F32).T.reshape((table.shape[1],) + (1,) * bucket.ndim + (table.shape[0],))
    out = jnp.zeros((table.shape[1],) + bucket.shape, F32)
    for b in range(table.shape[0]):
        out = jnp.where(bucket[None] == b, cols[..., b], out)
    return out


def _in_proj_kernel(x_ref, g_ref, w_ref, c_ref, o_ref, h_ref):
    @pl.when(pl.program_id(1) == 0)
    def _():
        h_ref[...] = _rms(x_ref[...], g_ref[...]).astype(BF16)

    y = jnp.dot(h_ref[...], w_ref[...], preferred_element_type=F32)
    o_ref[...] = (y * c_ref[...]).astype(o_ref.dtype)


def _in_proj(x2, g, w, col_scale):
    t, d = x2.shape
    n = w.shape[1]
    tm = _pick(t, (1024, 512, 256, 128))
    tn = _pick(n, (1024, 768, 512, 384, 256, 128))
    return pl.pallas_call(
        _in_proj_kernel,
        grid=(t // tm, n // tn),
        in_specs=[
            pl.BlockSpec((tm, d), lambda i, j: (i, 0)),
            pl.BlockSpec((1, d), lambda i, j: (0, 0)),
            pl.BlockSpec((d, tn), lambda i, j: (0, j)),
            pl.BlockSpec((1, tn), lambda i, j: (0, j)),
        ],
        out_specs=pl.BlockSpec((tm, tn), lambda i, j: (i, j)),
        out_shape=jax.ShapeDtypeStruct((t, n), BF16),
        scratch_shapes=[pltpu.VMEM((tm, d), BF16)],
        compiler_params=pltpu.CompilerParams(
            dimension_semantics=("parallel", "arbitrary"), vmem_limit_bytes=VMEM_LIMIT),
        name="in_proj",
    )(x2, g, w, col_scale)


def _dilated_plan(s_len):
    plans = []
    for window, dil in sorted(DILATED_CONFIGS, key=lambda c: -c[1]):
        half = window // (2 * dil)
        sub_len = s_len // dil
        assert s_len % dil == 0 and dil & (dil - 1) == 0
        if sub_len <= 2 * LANES:
            blk, nkeys, nblocks, whole = sub_len, sub_len, dil, True
        else:
            blk, nkeys, nblocks, whole = 2 * half, 4 * half, s_len // (2 * half), False
            assert sub_len % blk == 0 and sub_len // blk >= 2
        assert blk % 8 == 0 and nkeys % LANES == 0
        group = _pick(nblocks, [g for g in (8, 4, 2, 1) if g * blk * nkeys <= 8 * LANES * 2 * LANES])
        plans.append(dict(dil=dil, half=half, sub_len=sub_len, blk=blk, nkeys=nkeys, nblocks=nblocks,
                          whole=whole, group=group))
    return plans


def _dilated_kernel(q_ref, k_ref, v_ref, *rest, plans, pad, pad16):
    bias_refs = rest[:len(plans)]
    o_ref, qf, kf, vf, k16, v16, acc, m_st, l_st = rest[len(plans):]
    s_len = q_ref.shape[1]
    scale = 1.0 / math.sqrt(HEAD_DIM)

    qf[...] = q_ref[0].astype(F32)
    for buf, src in ((kf, k_ref), (vf, v_ref)):
        if pad:
            buf[pl.ds(0, pad), :] = jnp.zeros((pad, HEAD_DIM), F32)
            buf[pl.ds(pad + s_len, pad), :] = jnp.zeros((pad, HEAD_DIM), F32)
        buf[pl.ds(pad, s_len), :] = src[0].astype(F32)
    for buf, src in ((k16, k_ref), (v16, v_ref)):
        buf[pl.ds(0, pad16), :] = jnp.zeros((pad16, HEAD_DIM), BF16)
        buf[pl.ds(pad16 + s_len, pad16), :] = jnp.zeros((pad16, HEAD_DIM), BF16)
        buf[pl.ds(pad16, s_len), :] = src[0]

    for ci, plan in enumerate(plans):
        dil, half, blk, nkeys = plan["dil"], plan["half"], plan["blk"], plan["nkeys"]
        whole, group = plan["whole"], plan["group"]
        shift = dil.bit_length() - 1
        nblk_phase = plan["sub_len"] // blk
        bias_ref = bias_refs[ci]

        def scores(it, ci=ci, dil=dil, half=half, blk=blk, nkeys=nkeys, whole=whole, shift=shift,
                   nblk_phase=nblk_phase, bias_ref=bias_ref):
            if whole:
                phase, sub0, key0, variant = it, 0, 0, 0
            else:
                phase, n = it & (dil - 1), it >> shift
                sub0 = n * blk
                key0 = sub0 - half
                variant = jnp.where(n == 0, 1, jnp.where(n == nblk_phase - 1, 2, 0))
            if dil == 1:
                q_rows = pl.ds(pl.multiple_of(sub0, blk), blk)
                kv_rows = pl.ds(pl.multiple_of(pad16 + key0, half), nkeys)
                qb, kb, vb = q_ref[0, q_rows, :], k16[kv_rows, :], v16[kv_rows, :]
            else:
                q_rows = pl.ds(phase + dil * sub0, blk, stride=dil)
                kv_rows = pl.ds(pad + phase + dil * key0, nkeys, stride=dil)
                qb, kb, vb = (qf[q_rows, :].astype(BF16), kf[kv_rows, :].astype(BF16),
                              vf[kv_rows, :].astype(BF16))
            old = (m_st[q_rows, :], l_st[q_rows, :], acc[q_rows, :]) if ci > 0 else None
            s = lax.dot_general(qb, kb, (((1,), (1,)), ((), ())), preferred_element_type=F32)
            return q_rows, vb, old, s * scale + bias_ref[0, variant]

        def softmax(old, s, ci=ci, blk=blk, nkeys=nkeys):
            m_cur = jnp.max(s, axis=1, keepdims=True)
            if ci == 0:
                m_new = jnp.broadcast_to(m_cur, (blk, LANES))
            else:
                m_new = jnp.maximum(old[0], m_cur)
            p = jnp.exp(s - jnp.concatenate([m_new] * (nkeys // LANES), axis=1))
            return m_new, jnp.sum(p, axis=1, keepdims=True), p.astype(BF16)

        def values(old, vb, m_new, l_new, p, ci=ci, blk=blk):
            a_new = jnp.dot(p, vb, preferred_element_type=F32)
            if ci == 0:
                return jnp.broadcast_to(l_new, (blk, LANES)), a_new
            alpha = jnp.exp(old[0] - m_new)
            return alpha * old[1] + l_new, alpha * old[2] + a_new

        def body(j, carry, scores=scores, softmax=softmax, values=values, group=group):
            staged = [scores(j * group + g) for g in range(group)]
            probs = [softmax(old, s) for _, _, old, s in staged]
            for (q_rows, vb, old, _), (m_new, l_new, p) in zip(staged, probs):
                l_new, a_new = values(old, vb, m_new, l_new, p)
                m_st[q_rows, :] = m_new
                l_st[q_rows, :] = l_new
                acc[q_rows, :] = a_new
            return carry

        lax.fori_loop(0, plan["nblocks"] // group, body, 0)

    o_ref[0] = (acc[...] / l_st[...]).astype(o_ref.dtype)


def _dilated_bias(rel_bias, n_heads, plans):
    tables = []
    for plan in plans:
        half, blk, nkeys = plan["half"], plan["blk"], plan["nkeys"]
        key0 = 0 if plan["whole"] else -half
        col = jnp.arange(nkeys)[None, :]
        rel = (col + key0) - jnp.arange(blk)[:, None]
        b = _bias_lookup(rel_bias[:, :n_heads], _rel_bucket(rel * plan["dil"], rel_bias.shape[0]))
        band = jnp.abs(rel) <= half
        if plan["whole"]:
            masks = [band]
        else:
            masks = [band, band & (col >= half), band & (col < blk + half)]
        tables.append(jnp.stack([jnp.where(mk[None], b, NEG_INF) for mk in masks], axis=1))
    return tables


def _dilated_attn(proj3, biases, n_heads, plans):
    b, s_len, _ = proj3.shape
    pad = max([p["half"] * p["dil"] for p in plans if not p["whole"] and p["dil"] > 1], default=0)
    pad16 = max([p["half"] for p in plans if p["dil"] == 1], default=16)
    pad16 = -(-pad16 // 16) * 16
    kern = functools.partial(_dilated_kernel, plans=plans, pad=pad, pad16=pad16)
    col = lambda off: (lambda bi, h: (bi, 0, off + h))
    return pl.pallas_call(
        kern,
        grid=(b, n_heads),
        in_specs=[
            pl.BlockSpec((1, s_len, HEAD_DIM), col(0)),
            pl.BlockSpec((1, s_len, HEAD_DIM), col(n_heads)),
            pl.BlockSpec((1, s_len, HEAD_DIM), col(2 * n_heads)),
        ] + [pl.BlockSpec((1,) + t.shape[1:], lambda bi, h: (h, 0, 0, 0)) for t in biases],
        out_specs=pl.BlockSpec((1, s_len, HEAD_DIM), lambda bi, h: (bi, 0, h)),
        out_shape=jax.ShapeDtypeStruct((b, s_len, n_heads * HEAD_DIM), F32),
        scratch_shapes=[
            pltpu.VMEM((s_len, HEAD_DIM), F32),
            pltpu.VMEM((s_len + 2 * pad, HEAD_DIM), F32),
            pltpu.VMEM((s_len + 2 * pad, HEAD_DIM), F32),
            pltpu.VMEM((s_len + 2 * pad16, HEAD_DIM), BF16),
            pltpu.VMEM((s_len + 2 * pad16, HEAD_DIM), BF16),
            pltpu.VMEM((s_len, HEAD_DIM), F32),
            pltpu.VMEM((s_len, LANES), F32),
            pltpu.VMEM((s_len, LANES), F32),
        ],
        compiler_params=pltpu.CompilerParams(
            dimension_semantics=("parallel", "parallel"), vmem_limit_bytes=VMEM_LIMIT),
        name="dilated_attn",
    )(proj3, proj3, proj3, *biases)


def _diff_attn_kernel(q_ref, k_ref, v_ref, bias_ref, lq1_ref, lk1_ref, lq2_ref, lk2_ref, g_ref,
                      o_ref, qs0_ref, qs1_ref, s0_ref, s1_ref, m_ref, acc_ref, vtx_ref, *, tq, tk, heads,
                      lambda_init):
    s_len = k_ref.shape[1]
    nq, nk = s_len // tq, s_len // tk
    qsub, ksub = tq // LANES, tk // LANES
    n_tiles = heads * nq
    tile_shift = nq.bit_length() - 1
    dn_t = (((1,), (1,)), ((), ()))
    s_refs = (s0_ref, s1_ref)
    qs_refs = (qs0_ref, qs1_ref)
    assert nq % 2 == 0 and nq & (nq - 1) == 0

    eye = (lax.broadcasted_iota(jnp.int32, (HEAD_DIM, HEAD_DIM), 0)
           == lax.broadcasted_iota(jnp.int32, (HEAD_DIM, HEAD_DIM), 1)).astype(BF16)
    for hh in range(heads):
        for c in range(nk):
            vc = v_ref[0, pl.ds(c * tk, tk), pl.ds(hh * HEAD_DIM, HEAD_DIM)]
            vtx_ref[hh, pl.ds(0, HEAD_DIM), pl.ds(c * tk, tk)] = lax.dot_general(
                eye, vc, dn_t, preferred_element_type=F32).astype(BF16)
        vtx_ref[hh, pl.ds(HEAD_DIM, ONES_ROWS), :] = jnp.ones((ONES_ROWS, s_len), BF16)
    acc_ref[...] = jnp.zeros(acc_ref.shape, F32)
    lam = (jnp.exp(jnp.sum(lq1_ref[...] * lk1_ref[...], axis=-1, keepdims=True))
           - jnp.exp(jnp.sum(lq2_ref[...] * lk2_ref[...], axis=-1, keepdims=True))
           + lambda_init)

    def split(u):
        if isinstance(u, int):
            hh, t = divmod(u, nq)
            return hh, t, pl.ds(hh * HEAD_DIM, HEAD_DIM), pl.ds(t * tq, tq)
        hh, t = u >> tile_shift, u & (nq - 1)
        return (hh, t, pl.ds(pl.multiple_of(hh * HEAD_DIM, HEAD_DIM), HEAD_DIM),
                pl.ds(pl.multiple_of(t * tq, tq), tq))

    def start_tile(u, slot):
        _, _, lanes, rows = split(u)
        q = q_ref[0, rows, lanes]
        lane = lax.broadcasted_iota(jnp.int32, (tq, HEAD_DIM), 1)
        qs_refs[slot][pl.ds(0, tq), :] = jnp.where(lane < DIFF_QK_DIM, q, 0).astype(BF16)
        qs_refs[slot][pl.ds(tq, tq), :] = jnp.where(lane >= DIFF_QK_DIM, q, 0).astype(BF16)

    def score_chunk(u, c, slot):
        hh, t, lanes, _ = split(u)
        k0 = c * tk
        s = lax.dot_general(k_ref[0, pl.ds(k0, tk), lanes], qs_refs[slot][...], dn_t,
                            preferred_element_type=F32)
        bias_rows = []
        for cc in range(ksub):
            tiles = []
            for a in range(qsub):
                diff = (c * ksub + cc) - (t * qsub + a)
                idx = jnp.minimum(jnp.maximum(diff, -DIFF_BIAS_DMAX), DIFF_BIAS_DMAX) + DIFF_BIAS_DMAX
                tiles.append(bias_ref[hh, idx])
            bias_rows.append(jnp.concatenate(tiles + tiles, axis=1))
        s = s + jnp.concatenate(bias_rows, axis=0)
        s_refs[slot][pl.ds(k0, tk), :] = s
        m_chunk = jnp.max(s.reshape(tk // 8, 8, 2 * tq), axis=0)
        m_ref[0] = m_chunk if c == 0 else jnp.maximum(m_ref[0], m_chunk)

    def finish_scores(slot):
        m_ref[1 + slot] = jnp.broadcast_to(jnp.max(m_ref[0], axis=0, keepdims=True), (8, 2 * tq))

    def value_chunk(u, c, slot):
        hh = split(u)[0]
        k0 = c * tk
        p = jnp.exp2(s_refs[slot][pl.ds(k0, tk), :] - m_ref[1 + slot, pl.ds(0, 1), :]).astype(BF16)
        acc_ref[...] += jnp.dot(vtx_ref[hh, :, pl.ds(k0, tk)], p, preferred_element_type=F32)

    def finish_tile(u):
        _, _, lanes, rows = split(u)
        acc = acc_ref[...]
        o = acc[:HEAD_DIM] / acc[HEAD_DIM:HEAD_DIM + 1]
        od = o[:, :tq] - lam * o[:, tq:]
        ms = jnp.mean(od * od, axis=0, keepdims=True)
        y = jnp.transpose(od * lax.rsqrt(ms + EPS)) * g_ref[...] * (1.0 - lambda_init)
        o_ref[0, rows, lanes] = y.astype(o_ref.dtype)
        acc_ref[...] = jnp.zeros(acc_ref.shape, F32)

    def tile_step(u, slot):
        start_tile(u, slot)
        for c in range(nk):
            score_chunk(u, c, slot)
            if c == nk - 1:
                finish_scores(slot)
            value_chunk(u - 1, c, 1 - slot)
        finish_tile(u - 1)

    start_tile(0, 0)
    for c in range(nk):
        score_chunk(0, c, 0)
    finish_scores(0)

    def pair_step(w, carry):
        tile_step(2 * w + 1, 1)
        tile_step(2 * w + 2, 0)
        return carry

    lax.fori_loop(0, n_tiles // 2 - 1, pair_step, 0)
    tile_step(n_tiles - 1, 1)
    for c in range(nk):
        value_chunk(n_tiles - 1, c, 1)
    finish_tile(n_tiles - 1)


def _diff_bias(rel_bias, n_heads_a, n_heads_b):
    t = jnp.arange(2 * DIFF_BIAS_DMAX + 1) - DIFF_BIAS_DMAX
    rel = (LANES * t[:, None, None] + jnp.arange(LANES)[None, :, None] - jnp.arange(LANES)[None, None, :])
    table = rel_bias[:, n_heads_a:n_heads_a + n_heads_b].astype(F32) * LOG2_E
    return _bias_lookup(table, _rel_bucket(rel, rel_bias.shape[0]))


def _diff_attn(proj3, bias, lams, subln_g, n_heads_a, n_heads_b, lambda_init):
    b, s_len, _ = proj3.shape
    tq = _pick(s_len, (256, 128))
    tk = _pick(s_len, (256, 128))
    heads = _pick(n_heads_b, [h for h in (8, 4, 2, 1) if h * s_len <= DIFF_HEAD_ROWS])
    width = heads * HEAD_DIM
    base = 3 * n_heads_a // heads
    assert (3 * n_heads_a) % heads == 0
    kern = functools.partial(_diff_attn_kernel, tq=tq, tk=tk, heads=heads, lambda_init=lambda_init)
    vec = lambda n: pl.BlockSpec((1, n), lambda bi, h: (0, 0))
    col = lambda off: (lambda bi, h: (bi, 0, off + h))
    groups = n_heads_b // heads
    return pl.pallas_call(
        kern,
        grid=(b, groups),
        in_specs=[
            pl.BlockSpec((1, s_len, width), col(base)),
            pl.BlockSpec((1, s_len, width), col(base + groups)),
            pl.BlockSpec((1, s_len, width), col(base + 2 * groups)),
            pl.BlockSpec((heads,) + bias.shape[1:], lambda bi, h: (h, 0, 0, 0)),
            vec(DIFF_QK_DIM), vec(DIFF_QK_DIM), vec(DIFF_QK_DIM), vec(DIFF_QK_DIM),
            vec(HEAD_DIM),
        ],
        out_specs=pl.BlockSpec((1, s_len, width), col(0)),
        out_shape=jax.ShapeDtypeStruct((b, s_len, n_heads_b * HEAD_DIM), BF16),
        scratch_shapes=[
            pltpu.VMEM((2 * tq, HEAD_DIM), BF16),
            pltpu.VMEM((2 * tq, HEAD_DIM), BF16),
            pltpu.VMEM((s_len, 2 * tq), F32),
            pltpu.VMEM((s_len, 2 * tq), F32),
            pltpu.VMEM((3, 8, 2 * tq), F32),
            pltpu.VMEM((HEAD_DIM + ONES_ROWS, 2 * tq), F32),
            pltpu.VMEM((heads, HEAD_DIM + ONES_ROWS, s_len), BF16),
        ],
        compiler_params=pltpu.CompilerParams(
            dimension_semantics=("parallel", "parallel"), vmem_limit_bytes=VMEM_LIMIT),
        name="diff_attn",
    )(proj3, proj3, proj3, bias, *lams, subln_g)


def _out_proj_kernel(oa_ref, ob_ref, x_ref, wa_ref, wb_ref, ga_ref, gp_ref, o_ref):
    oa = _rms(oa_ref[...], ga_ref[...]).astype(BF16)
    y = jnp.dot(oa, wa_ref[...], preferred_element_type=F32)
    y = y + jnp.dot(ob_ref[...], wb_ref[...], preferred_element_type=F32)
    o_ref[...] = x_ref[...] + _rms(y, gp_ref[...])


def _out_proj(oa2, ob2, x2, wa, wb, ga, gp):
    t, d = x2.shape
    wa_w, wb_w = wa.shape[0], wb.shape[0]
    tm = _pick(t, (512, 256, 128))
    const = lambda i: (0, 0)
    row = lambda i: (i, 0)
    return pl.pallas_call(
        _out_proj_kernel,
        grid=(t // tm,),
        in_specs=[
            pl.BlockSpec((tm, wa_w), row),
            pl.BlockSpec((tm, wb_w), row),
            pl.BlockSpec((tm, d), row),
            pl.BlockSpec((wa_w, d), const),
            pl.BlockSpec((wb_w, d), const),
            pl.BlockSpec((1, wa_w), const),
            pl.BlockSpec((1, d), const),
        ],
        out_specs=pl.BlockSpec((tm, d), row),
        out_shape=jax.ShapeDtypeStruct((t, d), F32),
        compiler_params=pltpu.CompilerParams(
            dimension_semantics=("parallel",), vmem_limit_bytes=VMEM_LIMIT),
        name="out_proj",
    )(oa2, ob2, x2, wa, wb, ga, gp)


def _ffn_kernel(x_ref, gpre_ref, wg_ref, wu_ref, wd_ref, gpost_ref, o_ref, h_ref, acc_ref):
    j = pl.program_id(1)

    @pl.when(j == 0)
    def _():
        h_ref[...] = _rms(x_ref[...], gpre_ref[...]).astype(BF16)
        acc_ref[...] = jnp.zeros(acc_ref.shape, F32)

    h = h_ref[...]
    gate = jnp.dot(h, wg_ref[...], preferred_element_type=F32)
    up = jnp.dot(h, wu_ref[...], preferred_element_type=F32)
    f = gate * (1.0 / (1.0 + jnp.exp(-gate))) * up
    acc_ref[...] += jnp.dot(f.astype(BF16), wd_ref[...], preferred_element_type=F32)

    @pl.when(j == pl.num_programs(1) - 1)
    def _():
        o_ref[...] = x_ref[...] + _rms(acc_ref[...], gpost_ref[...])


def _ffn(x2, gpre, wg, wu, wd, gpost):
    t, d = x2.shape
    f = wg.shape[1]
    tm = _pick(t, (512, 256, 128))
    tf = _pick(f, (512, 256, 128))
    return pl.pallas_call(
        _ffn_kernel,
        grid=(t // tm, f // tf),
        in_specs=[
            pl.BlockSpec((tm, d), lambda i, j: (i, 0)),
            pl.BlockSpec((1, d), lambda i, j: (0, 0)),
            pl.BlockSpec((d, tf), lambda i, j: (0, j)),
            pl.BlockSpec((d, tf), lambda i, j: (0, j)),
            pl.BlockSpec((tf, d), lambda i, j: (j, 0)),
            pl.BlockSpec((1, d), lambda i, j: (0, 0)),
        ],
        out_specs=pl.BlockSpec((tm, d), lambda i, j: (i, 0)),
        out_shape=jax.ShapeDtypeStruct((t, d), F32),
        scratch_shapes=[pltpu.VMEM((tm, d), BF16), pltpu.VMEM((tm, d), F32)],
        compiler_params=pltpu.CompilerParams(
            dimension_semantics=("parallel", "arbitrary"), vmem_limit_bytes=VMEM_LIMIT),
        name="ffn",
    )(x2, gpre, wg, wu, wd, gpost)


def _encoder_layer(x, layer, p, bias_a, bias_b):
    b, s_len, d = x.shape
    width = p["w_in"].shape[1] // 6
    n_heads = width // HEAD_DIM
    lambda_init = 0.8 - 0.6 * math.exp(-0.3 * layer)
    row = lambda v: v.reshape(1, -1)

    x2 = x.reshape(b * s_len, d)
    col_scale = jnp.ones((1, 6 * width), F32).at[:, 3 * width:4 * width].set(DIFF_Q_SCALE)
    proj3 = _in_proj(x2, row(p["pre_mix_g"]), p["w_in"], col_scale).reshape(b, s_len, 6 * width)
    oa = _dilated_attn(proj3, bias_a[s_len], n_heads, _dilated_plan(s_len))
    lams = [row(p[k]) for k in ("lambda_q1", "lambda_k1", "lambda_q2", "lambda_k2")]
    ob = _diff_attn(proj3, bias_b, lams, row(p["subln_g"]), n_heads, n_heads, lambda_init)
    x1 = _out_proj(oa.reshape(b * s_len, width), ob.reshape(b * s_len, width), x2,
                   p["w_out"][:width], p["w_out"][width:], row(p["norm_a_g"]), row(p["post_mix_g"]))
    y = _ffn(x1, row(p["pre_ffn_g"]), p["w_gate"], p["w_up"], p["w_down"], row(p["post_ffn_g"]))
    return y.reshape(b, s_len, d)


def kernel(x_prompt, x_sample, pre_mix_g, post_mix_g, pre_ffn_g, post_ffn_g, w_in, w_out, norm_a_g,
           lambda_q1, lambda_k1, lambda_q2, lambda_k2, subln_g, w_gate, w_up, w_down, rel_bias):
    depth = w_in.shape[0]
    n_heads = w_in.shape[2] // 6 // HEAD_DIM
    bias_a = {x.shape[1]: _dilated_bias(rel_bias, n_heads, _dilated_plan(x.shape[1]))
              for x in (x_prompt, x_sample)}
    bias_b = _diff_bias(rel_bias, n_heads, n_heads)
    layers = []
    for layer in range(depth):
        layers.append(dict(
            pre_mix_g=pre_mix_g[layer], post_mix_g=post_mix_g[layer], pre_ffn_g=pre_ffn_g[layer],
            post_ffn_g=post_ffn_g[layer], norm_a_g=norm_a_g[layer], subln_g=subln_g[layer],
            lambda_q1=lambda_q1[layer], lambda_k1=lambda_k1[layer],
            lambda_q2=lambda_q2[layer], lambda_k2=lambda_k2[layer],
            w_in=w_in[layer].astype(BF16), w_out=w_out[layer].astype(BF16),
            w_gate=w_gate[layer].astype(BF16), w_up=w_up[layer].astype(BF16),
            w_down=w_down[layer].astype(BF16)))
    outs = []
    for x in (x_prompt, x_sample):
        for layer, p in enumerate(layers):
            x = _encoder_layer(x, layer, p, bias_a, bias_b)
        outs.append(x)
    return tuple(outs)
```

```python
import functools
import math

import jax
import jax.numpy as jnp
from jax import lax
from jax.experimental import pallas as pl
from jax.experimental.pallas import tpu as pltpu

F32 = jnp.float32
BF16 = jnp.bfloat16

EPS = 1e-6
NEG_INF = -1e30
HEAD_DIM = 128
DIFF_QK_DIM = 64
DILATED_CONFIGS = ((128, 1), (512, 4), (2048, 16))
MAX_DISTANCE = 1024
LANES = 128
VMEM_LIMIT = 56 * 1024 * 1024

DIFF_BIAS_DMAX = -(-(MAX_DISTANCE + LANES - 1) // LANES)
LOG2_E = math.log2(math.e)
DILATED_Q_SCALE = LOG2_E / math.sqrt(HEAD_DIM)
DIFF_Q_SCALE = LOG2_E / math.sqrt(DIFF_QK_DIM)
ONES_ROWS = 16
DIFF_HEAD_ROWS = 8192


def _pick(n, candidates):
    for c in candidates:
        if n % c == 0:
            return c
    raise ValueError(f"no tile in {candidates} divides {n}")


def _rms(x, g):
    ms = jnp.mean(x * x, axis=-1, keepdims=True)
    return x * lax.rsqrt(ms + EPS) * g


def _rel_bucket(rel, n_buckets):
    half = n_buckets // 2
    max_exact = half // 2
    n = jnp.abs(rel)
    nf = jnp.maximum(n, 1).astype(F32)
    large = max_exact + (jnp.log(nf / max_exact) / math.log(MAX_DISTANCE / max_exact)
                         * (half - max_exact)).astype(jnp.int32)
    large = jnp.minimum(large, half - 1)
    return jnp.where(rel > 0, half, 0) + jnp.where(n < max_exact, n, large)


def _bias_lookup(table, bucket):
    cols = table.astype(F32).T.reshape((table.shape[1],) + (1,) * bucket.ndim + (table.shape[0],))
    out = jnp.zeros((table.shape[1],) + bucket.shape, F32)
    for b in range(table.shape[0]):
        out = jnp.where(bucket[None] == b, cols[..., b], out)
    return out


def _in_proj_kernel(x_ref, g_ref, w_ref, c_ref, o_ref, h_ref):
    @pl.when(pl.program_id(1) == 0)
    def _():
        h_ref[...] = _rms(x_ref[...], g_ref[...]).astype(BF16)

    y = jnp.dot(h_ref[...], w_ref[...], preferred_element_type=F32)
    o_ref[...] = (y * c_ref[...]).astype(o_ref.dtype)


def _in_proj(x2, g, w, col_scale):
    t, d = x2.shape
    n = w.shape[1]
    tm = _pick(t, (1024, 512, 256, 128))
    tn = _pick(n, (1024, 768, 512, 384, 256, 128))
    return pl.pallas_call(
        _in_proj_kernel,
        grid=(t // tm, n // tn),
        in_specs=[
            pl.BlockSpec((tm, d), lambda i, j: (i, 0)),
            pl.BlockSpec((1, d), lambda i, j: (0, 0)),
            pl.BlockSpec((d, tn), lambda i, j: (0, j)),
            pl.BlockSpec((1, tn), lambda i, j: (0, j)),
        ],
        out_specs=pl.BlockSpec((tm, tn), lambda i, j: (i, j)),
        out_shape=jax.ShapeDtypeStruct((t, n), BF16),
        scratch_shapes=[pltpu.VMEM((tm, d), BF16)],
        compiler_params=pltpu.CompilerParams(
            dimension_semantics=("parallel", "arbitrary"), vmem_limit_bytes=VMEM_LIMIT),
        name="in_proj",
    )(x2, g, w, col_scale)


def _dilated_plan(s_len):
    plans = []
    for window, dil in sorted(DILATED_CONFIGS, key=lambda c: -c[1]):
        half = window // (2 * dil)
        sub_len = s_len // dil
        assert s_len % dil == 0 and dil & (dil - 1) == 0
        if sub_len <= 2 * LANES:
            blk, nkeys, nblocks, whole = sub_len, sub_len, dil, True
        else:
            blk, nkeys, nblocks, whole = 2 * half, 4 * half, s_len // (2 * half), False
            assert sub_len % blk == 0 and sub_len // blk >= 2
        assert blk % 8 == 0 and nkeys % LANES == 0
        group = _pick(nblocks, [g for g in (8, 4, 2, 1) if g * blk * nkeys <= 8 * LANES * 2 * LANES])
        plans.append(dict(dil=dil, half=half, sub_len=sub_len, blk=blk, nkeys=nkeys, nblocks=nblocks,
                          whole=whole, group=group))
    return plans


def _dilated_kernel(q_ref, k_ref, v_ref, *rest, plans, pad, pad16):
    bias_refs = rest[:len(plans)]
    o_ref, qf, kf, vf, k16, v16, acc, m_st, l_st = rest[len(plans):]
    s_len = q_ref.shape[1]

    qf[...] = q_ref[0].astype(F32)
    for buf, src in ((kf, k_ref), (vf, v_ref)):
        if pad:
            buf[pl.ds(0, pad), :] = jnp.zeros((pad, HEAD_DIM), F32)
            buf[pl.ds(pad + s_len, pad), :] = jnp.zeros((pad, HEAD_DIM), F32)
        buf[pl.ds(pad, s_len), :] = src[0].astype(F32)
    for buf, src in ((k16, k_ref), (v16, v_ref)):
        buf[pl.ds(0, pad16), :] = jnp.zeros((pad16, HEAD_DIM), BF16)
        buf[pl.ds(pad16 + s_len, pad16), :] = jnp.zeros((pad16, HEAD_DIM), BF16)
        buf[pl.ds(pad16, s_len), :] = src[0]

    for ci, plan in enumerate(plans):
        dil, half, blk, nkeys = plan["dil"], plan["half"], plan["blk"], plan["nkeys"]
        whole, group = plan["whole"], plan["group"]
        shift = dil.bit_length() - 1
        nblk_phase = plan["sub_len"] // blk
        bias_ref = bias_refs[ci]

        def scores(it, ci=ci, dil=dil, half=half, blk=blk, nkeys=nkeys, whole=whole, shift=shift,
                   nblk_phase=nblk_phase, bias_ref=bias_ref):
            if whole:
                phase, sub0, key0, variant = it, 0, 0, 0
            else:
                phase, n = it & (dil - 1), it >> shift
                sub0 = n * blk
                key0 = sub0 - half
                variant = jnp.where(n == 0, 1, jnp.where(n == nblk_phase - 1, 2, 0))
            if dil == 1:
                q_rows = pl.ds(pl.multiple_of(sub0, blk), blk)
                kv_rows = pl.ds(pl.multiple_of(pad16 + key0, half), nkeys)
                qb, kb, vb = q_ref[0, q_rows, :], k16[kv_rows, :], v16[kv_rows, :]
            else:
                q_rows = pl.ds(phase + dil * sub0, blk, stride=dil)
                kv_rows = pl.ds(pad + phase + dil * key0, nkeys, stride=dil)
                qb, kb, vb = (qf[q_rows, :].astype(BF16), kf[kv_rows, :].astype(BF16),
                              vf[kv_rows, :].astype(BF16))
            old = (m_st[q_rows, :], l_st[q_rows, :], acc[q_rows, :]) if ci > 0 else None
            s = lax.dot_general(qb, kb, (((1,), (1,)), ((), ())), preferred_element_type=F32)
            return q_rows, vb, old, s + bias_ref[0, variant]

        def softmax(old, s, ci=ci, blk=blk, nkeys=nkeys):
            m_cur = jnp.max(s, axis=1, keepdims=True)
            if ci == 0:
                m_new = jnp.broadcast_to(m_cur, (blk, LANES))
            else:
                m_new = jnp.maximum(old[0], m_cur)
            p = jnp.exp2(s - jnp.concatenate([m_new] * (nkeys // LANES), axis=1))
            return m_new, jnp.sum(p, axis=1, keepdims=True), p.astype(BF16)

        def values(old, vb, m_new, l_new, p, ci=ci, blk=blk):
            a_new = jnp.dot(p, vb, preferred_element_type=F32)
            if ci == 0:
                return jnp.broadcast_to(l_new, (blk, LANES)), a_new
            alpha = jnp.exp2(old[0] - m_new)
            return alpha * old[1] + l_new, alpha * old[2] + a_new

        def body(j, carry, scores=scores, softmax=softmax, values=values, group=group):
            staged = [scores(j * group + g) for g in range(group)]
            probs = [softmax(old, s) for _, _, old, s in staged]
            for (q_rows, vb, old, _), (m_new, l_new, p) in zip(staged, probs):
                l_new, a_new = values(old, vb, m_new, l_new, p)
                m_st[q_rows, :] = m_new
                l_st[q_rows, :] = l_new
                acc[q_rows, :] = a_new
            return carry

        lax.fori_loop(0, plan["nblocks"] // group, body, 0)

    o_ref[0] = (acc[...] / l_st[...]).astype(o_ref.dtype)


def _dilated_bias(rel_bias, n_heads, plans):
    tables = []
    for plan in plans:
        half, blk, nkeys = plan["half"], plan["blk"], plan["nkeys"]
        key0 = 0 if plan["whole"] else -half
        col = jnp.arange(nkeys)[None, :]
        rel = (col + key0) - jnp.arange(blk)[:, None]
        b = _bias_lookup(rel_bias[:, :n_heads].astype(F32) * LOG2_E,
                         _rel_bucket(rel * plan["dil"], rel_bias.shape[0]))
        band = jnp.abs(rel) <= half
        if plan["whole"]:
            masks = [band]
        else:
            masks = [band, band & (col >= half), band & (col < blk + half)]
        tables.append(jnp.stack([jnp.where(mk[None], b, NEG_INF) for mk in masks], axis=1))
    return tables


def _dilated_attn(proj3, biases, n_heads, plans):
    b, s_len, _ = proj3.shape
    pad = max([p["half"] * p["dil"] for p in plans if not p["whole"] and p["dil"] > 1], default=0)
    pad16 = max([p["half"] for p in plans if p["dil"] == 1], default=16)
    pad16 = -(-pad16 // 16) * 16
    kern = functools.partial(_dilated_kernel, plans=plans, pad=pad, pad16=pad16)
    col = lambda off: (lambda bi, h: (bi, 0, off + h))
    return pl.pallas_call(
        kern,
        grid=(b, n_heads),
        in_specs=[
            pl.BlockSpec((1, s_len, HEAD_DIM), col(0)),
            pl.BlockSpec((1, s_len, HEAD_DIM), col(n_heads)),
            pl.BlockSpec((1, s_len, HEAD_DIM), col(2 * n_heads)),
        ] + [pl.BlockSpec((1,) + t.shape[1:], lambda bi, h: (h, 0, 0, 0)) for t in biases],
        out_specs=pl.BlockSpec((1, s_len, HEAD_DIM), lambda bi, h: (bi, 0, h)),
        out_shape=jax.ShapeDtypeStruct((b, s_len, n_heads * HEAD_DIM), F32),
        scratch_shapes=[
            pltpu.VMEM((s_len, HEAD_DIM), F32),
            pltpu.VMEM((s_len + 2 * pad, HEAD_DIM), F32),
            pltpu.VMEM((s_len + 2 * pad, HEAD_DIM), F32),
            pltpu.VMEM((s_len + 2 * pad16, HEAD_DIM), BF16),
            pltpu.VMEM((s_len + 2 * pad16, HEAD_DIM), BF16),
            pltpu.VMEM((s_len, HEAD_DIM), F32),
            pltpu.VMEM((s_len, LANES), F32),
            pltpu.VMEM((s_len, LANES), F32),
        ],
        compiler_params=pltpu.CompilerParams(
            dimension_semantics=("parallel", "parallel"), vmem_limit_bytes=VMEM_LIMIT),
        name="dilated_attn",
    )(proj3, proj3, proj3, *biases)


def _diff_attn_kernel(q_ref, k_ref, v_ref, bias_ref, lq1_ref, lk1_ref, lq2_ref, lk2_ref, g_ref,
                      o_ref, qs0_ref, qs1_ref, s0_ref, s1_ref, m_ref, acc_ref, vtx_ref, *, tq, tk, heads,
                      lambda_init):
    s_len = k_ref.shape[1]
    nq, nk = s_len // tq, s_len // tk
    qsub, ksub = tq // LANES, tk // LANES
    n_tiles = heads * nq
    tile_shift = nq.bit_length() - 1
    dn_t = (((1,), (1,)), ((), ()))
    s_refs = (s0_ref, s1_ref)
    qs_refs = (qs0_ref, qs1_ref)
    assert nq % 2 == 0 and nq & (nq - 1) == 0

    eye = (lax.broadcasted_iota(jnp.int32, (HEAD_DIM, HEAD_DIM), 0)
           == lax.broadcasted_iota(jnp.int32, (HEAD_DIM, HEAD_DIM), 1)).astype(BF16)
    for hh in range(heads):
        for c in range(nk):
            vc = v_ref[0, pl.ds(c * tk, tk), pl.ds(hh * HEAD_DIM, HEAD_DIM)]
            vtx_ref[hh, pl.ds(0, HEAD_DIM), pl.ds(c * tk, tk)] = lax.dot_general(
                eye, vc, dn_t, preferred_element_type=F32).astype(BF16)
        vtx_ref[hh, pl.ds(HEAD_DIM, ONES_ROWS), :] = jnp.ones((ONES_ROWS, s_len), BF16)
    acc_ref[...] = jnp.zeros(acc_ref.shape, F32)
    lam = (jnp.exp(jnp.sum(lq1_ref[...] * lk1_ref[...], axis=-1, keepdims=True))
           - jnp.exp(jnp.sum(lq2_ref[...] * lk2_ref[...], axis=-1, keepdims=True))
           + lambda_init)

    def split(u):
        if isinstance(u, int):
            hh, t = divmod(u, nq)
            return hh, t, pl.ds(hh * HEAD_DIM, HEAD_DIM), pl.ds(t * tq, tq)
        hh, t = u >> tile_shift, u & (nq - 1)
        return (hh, t, pl.ds(pl.multiple_of(hh * HEAD_DIM, HEAD_DIM), HEAD_DIM),
                pl.ds(pl.multiple_of(t * tq, tq), tq))

    def start_tile(u, slot):
        _, _, lanes, rows = split(u)
        q = q_ref[0, rows, lanes]
        lane = lax.broadcasted_iota(jnp.int32, (tq, HEAD_DIM), 1)
        qs_refs[slot][pl.ds(0, tq), :] = jnp.where(lane < DIFF_QK_DIM, q, 0).astype(BF16)
        qs_refs[slot][pl.ds(tq, tq), :] = jnp.where(lane >= DIFF_QK_DIM, q, 0).astype(BF16)

    def score_chunk(u, c, slot):
        hh, t, lanes, _ = split(u)
        k0 = c * tk
        s = lax.dot_general(k_ref[0, pl.ds(k0, tk), lanes], qs_refs[slot][...], dn_t,
                            preferred_element_type=F32)
        bias_rows = []
        for cc in range(ksub):
            tiles = []
            for a in range(qsub):
                diff = (c * ksub + cc) - (t * qsub + a)
                idx = jnp.minimum(jnp.maximum(diff, -DIFF_BIAS_DMAX), DIFF_BIAS_DMAX) + DIFF_BIAS_DMAX
                tiles.append(bias_ref[hh, idx])
            bias_rows.append(jnp.concatenate(tiles + tiles, axis=1))
        s = s + jnp.concatenate(bias_rows, axis=0)
        s_refs[slot][pl.ds(k0, tk), :] = s
        m_chunk = jnp.max(s.reshape(tk // 8, 8, 2 * tq), axis=0)
        m_ref[0] = m_chunk if c == 0 else jnp.maximum(m_ref[0], m_chunk)

    def finish_scores(slot):
        m_ref[1 + slot] = jnp.broadcast_to(jnp.max(m_ref[0], axis=0, keepdims=True), (8, 2 * tq))

    def value_chunk(u, c, slot):
        hh = split(u)[0]
        k0 = c * tk
        p = jnp.exp2(s_refs[slot][pl.ds(k0, tk), :] - m_ref[1 + slot, pl.ds(0, 1), :]).astype(BF16)
        acc_ref[...] += jnp.dot(vtx_ref[hh, :, pl.ds(k0, tk)], p, preferred_element_type=F32)

    def finish_tile(u):
        _, _, lanes, rows = split(u)
        acc = acc_ref[...]
        o = acc[:HEAD_DIM] / acc[HEAD_DIM:HEAD_DIM + 1]
        od = o[:, :tq] - lam * o[:, tq:]
        ms = jnp.mean(od * od, axis=0, keepdims=True)
        y = jnp.transpose(od * lax.rsqrt(ms + EPS)) * g_ref[...] * (1.0 - lambda_init)
        o_ref[0, rows, lanes] = y.astype(o_ref.dtype)
        acc_ref[...] = jnp.zeros(acc_ref.shape, F32)

    def tile_step(u, slot):
        start_tile(u, slot)
        for c in range(nk):
            score_chunk(u, c, slot)
            if c == nk - 1:
                finish_scores(slot)
            value_chunk(u - 1, c, 1 - slot)
        finish_tile(u - 1)

    start_tile(0, 0)
    for c in range(nk):
        score_chunk(0, c, 0)
    finish_scores(0)

    def pair_step(w, carry):
        tile_step(2 * w + 1, 1)
        tile_step(2 * w + 2, 0)
        return carry

    lax.fori_loop(0, n_tiles // 2 - 1, pair_step, 0)
    tile_step(n_tiles - 1, 1)
    for c in range(nk):
        value_chunk(n_tiles - 1, c, 1)
    finish_tile(n_tiles - 1)


def _diff_bias(rel_bias, n_heads_a, n_heads_b):
    t = jnp.arange(2 * DIFF_BIAS_DMAX + 1) - DIFF_BIAS_DMAX
    rel = (LANES * t[:, None, None] + jnp.arange(LANES)[None, :, None] - jnp.arange(LANES)[None, None, :])
    table = rel_bias[:, n_heads_a:n_heads_a + n_heads_b].astype(F32) * LOG2_E
    return _bias_lookup(table, _rel_bucket(rel, rel_bias.shape[0]))


def _diff_attn(proj3, bias, lams, subln_g, n_heads_a, n_heads_b, lambda_init):
    b, s_len, _ = proj3.shape
    tq = _pick(s_len, (256, 128))
    tk = _pick(s_len, (256, 128))
    heads = _pick(n_heads_b, [h for h in (8, 4, 2, 1) if h * s_len <= DIFF_HEAD_ROWS])
    width = heads * HEAD_DIM
    base = 3 * n_heads_a // heads
    assert (3 * n_heads_a) % heads == 0
    kern = functools.partial(_diff_attn_kernel, tq=tq, tk=tk, heads=heads, lambda_init=lambda_init)
    vec = lambda n: pl.BlockSpec((1, n), lambda bi, h: (0, 0))
    col = lambda off: (lambda bi, h: (bi, 0, off + h))
    groups = n_heads_b // heads
    return pl.pallas_call(
        kern,
        grid=(b, groups),
        in_specs=[
            pl.BlockSpec((1, s_len, width), col(base)),
            pl.BlockSpec((1, s_len, width), col(base + groups)),
            pl.BlockSpec((1, s_len, width), col(base + 2 * groups)),
            pl.BlockSpec((heads,) + bias.shape[1:], lambda bi, h: (h, 0, 0, 0)),
            vec(DIFF_QK_DIM), vec(DIFF_QK_DIM), vec(DIFF_QK_DIM), vec(DIFF_QK_DIM),
            vec(HEAD_DIM),
        ],
        out_specs=pl.BlockSpec((1, s_len, width), col(0)),
        out_shape=jax.ShapeDtypeStruct((b, s_len, n_heads_b * HEAD_DIM), BF16),
        scratch_shapes=[
            pltpu.VMEM((2 * tq, HEAD_DIM), BF16),
            pltpu.VMEM((2 * tq, HEAD_DIM), BF16),
            pltpu.VMEM((s_len, 2 * tq), F32),
            pltpu.VMEM((s_len, 2 * tq), F32),
            pltpu.VMEM((3, 8, 2 * tq), F32),
            pltpu.VMEM((HEAD_DIM + ONES_ROWS, 2 * tq), F32),
            pltpu.VMEM((heads, HEAD_DIM + ONES_ROWS, s_len), BF16),
        ],
        compiler_params=pltpu.CompilerParams(
            dimension_semantics=("parallel", "parallel"), vmem_limit_bytes=VMEM_LIMIT),
        name="diff_attn",
    )(proj3, proj3, proj3, bias, *lams, subln_g)


def _out_proj_kernel(oa_ref, ob_ref, x_ref, wa_ref, wb_ref, ga_ref, gp_ref, o_ref):
    oa = _rms(oa_ref[...], ga_ref[...]).astype(BF16)
    y = jnp.dot(oa, wa_ref[...], preferred_element_type=F32)
    y = y + jnp.dot(ob_ref[...], wb_ref[...], preferred_element_type=F32)
    o_ref[...] = x_ref[...] + _rms(y, gp_ref[...])


def _out_proj(oa2, ob2, x2, wa, wb, ga, gp):
    t, d = x2.shape
    wa_w, wb_w = wa.shape[0], wb.shape[0]
    tm = _pick(t, (512, 256, 128))
    const = lambda i: (0, 0)
    row = lambda i: (i, 0)
    return pl.pallas_call(
        _out_proj_kernel,
        grid=(t // tm,),
        in_specs=[
            pl.BlockSpec((tm, wa_w), row),
            pl.BlockSpec((tm, wb_w), row),
            pl.BlockSpec((tm, d), row),
            pl.BlockSpec((wa_w, d), const),
            pl.BlockSpec((wb_w, d), const),
            pl.BlockSpec((1, wa_w), const),
            pl.BlockSpec((1, d), const),
        ],
        out_specs=pl.BlockSpec((tm, d), row),
        out_shape=jax.ShapeDtypeStruct((t, d), F32),
        compiler_params=pltpu.CompilerParams(
            dimension_semantics=("parallel",), vmem_limit_bytes=VMEM_LIMIT),
        name="out_proj",
    )(oa2, ob2, x2, wa, wb, ga, gp)


def _ffn_kernel(x_ref, xn_ref, gpre_ref, wg_ref, wu_ref, wd_ref, gpost_ref, o_ref, h_ref, acc_ref):
    i, j = pl.program_id(0), pl.program_id(1)
    slot = i & 1

    @pl.when((i == 0) & (j == 0))
    def _():
        h_ref[0] = _rms(x_ref[...], gpre_ref[...]).astype(BF16)
        acc_ref[...] = jnp.zeros(acc_ref.shape, F32)

    def step(last):
        h = h_ref[slot]
        gate = jnp.dot(h, wg_ref[...], preferred_element_type=F32)
        up = jnp.dot(h, wu_ref[...], preferred_element_type=F32)
        f = gate * (1.0 / (1.0 + jnp.exp(-gate))) * up
        part = jnp.dot(f.astype(BF16), wd_ref[...], preferred_element_type=F32)
        if not last:
            acc_ref[...] += part
            return
        h_ref[1 - slot] = _rms(xn_ref[...], gpre_ref[...]).astype(BF16)
        o_ref[...] = x_ref[...] + _rms(acc_ref[...] + part, gpost_ref[...])
        acc_ref[...] = jnp.zeros(acc_ref.shape, F32)

    is_last = j == pl.num_programs(1) - 1
    pl.when(jnp.logical_not(is_last))(functools.partial(step, False))
    pl.when(is_last)(functools.partial(step, True))


def _ffn(x2, gpre, wg, wu, wd, gpost):
    t, d = x2.shape
    f = wg.shape[1]
    tm = _pick(t, (512, 256, 128))
    tf = _pick(f, (512, 256, 128))
    n_rows = t // tm
    return pl.pallas_call(
        _ffn_kernel,
        grid=(n_rows, f // tf),
        in_specs=[
            pl.BlockSpec((tm, d), lambda i, j: (i, 0)),
            pl.BlockSpec((tm, d), lambda i, j: (jnp.minimum(i + 1, n_rows - 1), 0)),
            pl.BlockSpec((1, d), lambda i, j: (0, 0)),
            pl.BlockSpec((d, tf), lambda i, j: (0, j)),
            pl.BlockSpec((d, tf), lambda i, j: (0, j)),
            pl.BlockSpec((tf, d), lambda i, j: (j, 0)),
            pl.BlockSpec((1, d), lambda i, j: (0, 0)),
        ],
        out_specs=pl.BlockSpec((tm, d), lambda i, j: (i, 0)),
        out_shape=jax.ShapeDtypeStruct((t, d), F32),
        scratch_shapes=[pltpu.VMEM((2, tm, d), BF16), pltpu.VMEM((tm, d), F32)],
        compiler_params=pltpu.CompilerParams(
            dimension_semantics=("arbitrary", "arbitrary"), vmem_limit_bytes=VMEM_LIMIT),
        name="ffn",
    )(x2, x2, gpre, wg, wu, wd, gpost)


def _encoder_layer(x, layer, p, bias_a, bias_b):
    b, s_len, d = x.shape
    width = p["w_in"].shape[1] // 6
    n_heads = width // HEAD_DIM
    lambda_init = 0.8 - 0.6 * math.exp(-0.3 * layer)
    row = lambda v: v.reshape(1, -1)

    x2 = x.reshape(b * s_len, d)
    col_scale = (jnp.ones((1, 6 * width), F32).at[:, :width].set(DILATED_Q_SCALE)
                 .at[:, 3 * width:4 * width].set(DIFF_Q_SCALE))
    proj3 = _in_proj(x2, row(p["pre_mix_g"]), p["w_in"], col_scale).reshape(b, s_len, 6 * width)
    oa = _dilated_attn(proj3, bias_a[s_len], n_heads, _dilated_plan(s_len))
    lams = [row(p[k]) for k in ("lambda_q1", "lambda_k1", "lambda_q2", "lambda_k2")]
    ob = _diff_attn(proj3, bias_b, lams, row(p["subln_g"]), n_heads, n_heads, lambda_init)
    x1 = _out_proj(oa.reshape(b * s_len, width), ob.reshape(b * s_len, width), x2,
                   p["w_out"][:width], p["w_out"][width:], row(p["norm_a_g"]), row(p["post_mix_g"]))
    y = _ffn(x1, row(p["pre_ffn_g"]), p["w_gate"], p["w_up"], p["w_down"], row(p["post_ffn_g"]))
    return y.reshape(b, s_len, d)


def kernel(x_prompt, x_sample, pre_mix_g, post_mix_g, pre_ffn_g, post_ffn_g, w_in, w_out, norm_a_g,
           lambda_q1, lambda_k1, lambda_q2, lambda_k2, subln_g, w_gate, w_up, w_down, rel_bias):
    depth = w_in.shape[0]
    n_heads = w_in.shape[2] // 6 // HEAD_DIM
    bias_a = {x.shape[1]: _dilated_bias(rel_bias, n_heads, _dilated_plan(x.shape[1]))
              for x in (x_prompt, x_sample)}
    bias_b = _diff_bias(rel_bias, n_heads, n_heads)
    layers = []
    for layer in range(depth):
        layers.append(dict(
            pre_mix_g=pre_mix_g[layer], post_mix_g=post_mix_g[layer], pre_ffn_g=pre_ffn_g[layer],
            post_ffn_g=post_ffn_g[layer], norm_a_g=norm_a_g[layer], subln_g=subln_g[layer],
            lambda_q1=lambda_q1[layer], lambda_k1=lambda_k1[layer],
            lambda_q2=lambda_q2[layer], lambda_k2=lambda_k2[layer],
            w_in=w_in[layer].astype(BF16), w_out=w_out[layer].astype(BF16),
            w_gate=w_gate[layer].astype(BF16), w_up=w_up[layer].astype(BF16),
            w_down=w_down[layer].astype(BF16)))
    outs = []
    for x in (x_prompt, x_sample):
        for layer, p in enumerate(layers):
            x = _encoder_layer(x, layer, p, bias_a, bias_b)
        outs.append(x)
    return tuple(outs)
```

```python
import functools
import math

import jax
import jax.numpy as jnp
from jax import lax
from jax.experimental import pallas as pl
from jax.experimental.pallas import tpu as pltpu

F32 = jnp.float32
BF16 = jnp.bfloat16

EPS = 1e-6
NEG_INF = -1e30
HEAD_DIM = 128
DIFF_QK_DIM = 64
DILATED_CONFIGS = ((128, 1), (512, 4), (2048, 16))
MAX_DISTANCE = 1024
LANES = 128
VMEM_LIMIT = 56 * 1024 * 1024

DIFF_BIAS_DMAX = -(-(MAX_DISTANCE + LANES - 1) // LANES)
LOG2_E = math.log2(math.e)
DILATED_Q_SCALE = LOG2_E / math.sqrt(HEAD_DIM)
DIFF_Q_SCALE = LOG2_E / math.sqrt(DIFF_QK_DIM)
ONES_ROWS = 16
DIFF_HEAD_ROWS = 8192


def _pick(n, candidates):
    for c in candidates:
        if n % c == 0:
            return c
    raise ValueError(f"no tile in {candidates} divides {n}")


def _rms(x, g):
    ms = jnp.mean(x * x, axis=-1, keepdims=True)
    return x * lax.rsqrt(ms + EPS) * g


def _rel_bucket(rel, n_buckets):
    half = n_buckets // 2
    max_exact = half // 2
    n = jnp.abs(rel)
    nf = jnp.maximum(n, 1).astype(F32)
    large = max_exact + (jnp.log(nf / max_exact) / math.log(MAX_DISTANCE / max_exact)
                         * (half - max_exact)).astype(jnp.int32)
    large = jnp.minimum(large, half - 1)
    return jnp.where(rel > 0, half, 0) + jnp.where(n < max_exact, n, large)


def _bias_lookup(table, bucket):
    cols = table.astype(F32).T.reshape((table.shape[1],) + (1,) * bucket.ndim + (table.shape[0],))
    out = jnp.zeros((table.shape[1],) + bucket.shape, F32)
    for b in range(table.shape[0]):
        out = jnp.where(bucket[None] == b, cols[..., b], out)
    return out


def _in_proj_kernel(x_ref, g_ref, w_ref, c_ref, o_ref, h_ref):
    @pl.when(pl.program_id(1) == 0)
    def _():
        h_ref[...] = _rms(x_ref[...], g_ref[...]).astype(BF16)

    y = jnp.dot(h_ref[...], w_ref[...], preferred_element_type=F32)
    o_ref[...] = (y * c_ref[...]).astype(o_ref.dtype)


def _in_proj(x2, g, w, col_scale):
    t, d = x2.shape
    n = w.shape[1]
    tm = _pick(t, (1024, 512, 256, 128))
    tn = _pick(n, (1024, 768, 512, 384, 256, 128))
    return pl.pallas_call(
        _in_proj_kernel,
        grid=(t // tm, n // tn),
        in_specs=[
            pl.BlockSpec((tm, d), lambda i, j: (i, 0)),
            pl.BlockSpec((1, d), lambda i, j: (0, 0)),
            pl.BlockSpec((d, tn), lambda i, j: (0, j)),
            pl.BlockSpec((1, tn), lambda i, j: (0, j)),
        ],
        out_specs=pl.BlockSpec((tm, tn), lambda i, j: (i, j)),
        out_shape=jax.ShapeDtypeStruct((t, n), BF16),
        scratch_shapes=[pltpu.VMEM((tm, d), BF16)],
        compiler_params=pltpu.CompilerParams(
            dimension_semantics=("parallel", "arbitrary"), vmem_limit_bytes=VMEM_LIMIT),
        name="in_proj",
    )(x2, g, w, col_scale)


def _dilated_plan(s_len):
    plans = []
    for window, dil in sorted(DILATED_CONFIGS, key=lambda c: -c[1]):
        half = window // (2 * dil)
        sub_len = s_len // dil
        assert s_len % dil == 0 and dil & (dil - 1) == 0
        if sub_len <= 2 * LANES:
            blk, nkeys, nblocks, whole = sub_len, sub_len, dil, True
        else:
            blk, nkeys, nblocks, whole = 2 * half, 4 * half, s_len // (2 * half), False
            assert sub_len % blk == 0 and sub_len // blk >= 2
        assert blk % 8 == 0 and nkeys % LANES == 0
        group = _pick(nblocks, [g for g in (8, 4, 2, 1) if g * blk * nkeys <= 8 * LANES * 2 * LANES])
        plans.append(dict(dil=dil, half=half, sub_len=sub_len, blk=blk, nkeys=nkeys, nblocks=nblocks,
                          whole=whole, group=group))
    return plans


def _dilated_kernel(q_ref, k_ref, v_ref, *rest, plans, pad, pad16):
    bias_refs = rest[:len(plans)]
    o_ref, qf, kf, vf, k16, v16, acc, m_st, l_st = rest[len(plans):]
    s_len = q_ref.shape[1]

    qf[...] = q_ref[0].astype(F32)
    for buf, src in ((kf, k_ref), (vf, v_ref)):
        if pad:
            buf[pl.ds(0, pad), :] = jnp.zeros((pad, HEAD_DIM), F32)
            buf[pl.ds(pad + s_len, pad), :] = jnp.zeros((pad, HEAD_DIM), F32)
        buf[pl.ds(pad, s_len), :] = src[0].astype(F32)
    for buf, src in ((k16, k_ref), (v16, v_ref)):
        buf[pl.ds(0, pad16), :] = jnp.zeros((pad16, HEAD_DIM), BF16)
        buf[pl.ds(pad16 + s_len, pad16), :] = jnp.zeros((pad16, HEAD_DIM), BF16)
        buf[pl.ds(pad16, s_len), :] = src[0]

    for ci, plan in enumerate(plans):
        dil, half, blk, nkeys = plan["dil"], plan["half"], plan["blk"], plan["nkeys"]
        whole, group = plan["whole"], plan["group"]
        shift = dil.bit_length() - 1
        nblk_phase = plan["sub_len"] // blk
        bias_ref = bias_refs[ci]

        def scores(it, ci=ci, dil=dil, half=half, blk=blk, nkeys=nkeys, whole=whole, shift=shift,
                   nblk_phase=nblk_phase, bias_ref=bias_ref):
            if whole:
                phase, sub0, key0, variant = it, 0, 0, 0
            else:
                phase, n = it & (dil - 1), it >> shift
                sub0 = n * blk
                key0 = sub0 - half
                variant = jnp.where(n == 0, 1, jnp.where(n == nblk_phase - 1, 2, 0))
            if dil == 1:
                q_rows = pl.ds(pl.multiple_of(sub0, blk), blk)
                kv_rows = pl.ds(pl.multiple_of(pad16 + key0, half), nkeys)
                qb, kb, vb = q_ref[0, q_rows, :], k16[kv_rows, :], v16[kv_rows, :]
            else:
                q_rows = pl.ds(phase + dil * sub0, blk, stride=dil)
                kv_rows = pl.ds(pad + phase + dil * key0, nkeys, stride=dil)
                qb, kb, vb = (qf[q_rows, :].astype(BF16), kf[kv_rows, :].astype(BF16),
                              vf[kv_rows, :].astype(BF16))
            old = (m_st[q_rows, :], l_st[q_rows, :], acc[q_rows, :]) if ci > 0 else None
            s = lax.dot_general(qb, kb, (((1,), (1,)), ((), ())), preferred_element_type=F32)
            return q_rows, vb, old, s + bias_ref[0, variant]

        def softmax(old, s, ci=ci, blk=blk, nkeys=nkeys):
            m_cur = jnp.max(s, axis=1, keepdims=True)
            if ci == 0:
                m_new = jnp.broadcast_to(m_cur, (blk, LANES))
            else:
                m_new = jnp.maximum(old[0], m_cur)
            p = jnp.exp2(s - jnp.concatenate([m_new] * (nkeys // LANES), axis=1))
            return m_new, jnp.sum(p, axis=1, keepdims=True), p.astype(BF16)

        def values(old, vb, m_new, l_new, p, ci=ci, blk=blk):
            a_new = jnp.dot(p, vb, preferred_element_type=F32)
            if ci == 0:
                return jnp.broadcast_to(l_new, (blk, LANES)), a_new
            alpha = jnp.exp2(old[0] - m_new)
            return alpha * old[1] + l_new, alpha * old[2] + a_new

        def body(j, carry, scores=scores, softmax=softmax, values=values, group=group):
            staged = [scores(j * group + g) for g in range(group)]
            probs = [softmax(old, s) for _, _, old, s in staged]
            for (q_rows, vb, old, _), (m_new, l_new, p) in zip(staged, probs):
                l_new, a_new = values(old, vb, m_new, l_new, p)
                m_st[q_rows, :] = m_new
                l_st[q_rows, :] = l_new
                acc[q_rows, :] = a_new
            return carry

        lax.fori_loop(0, plan["nblocks"] // group, body, 0)

    o_ref[0] = (acc[...] / l_st[...]).astype(o_ref.dtype)


def _dilated_bias(rel_bias, n_heads, plans):
    tables = []
    for plan in plans:
        half, blk, nkeys = plan["half"], plan["blk"], plan["nkeys"]
        key0 = 0 if plan["whole"] else -half
        col = jnp.arange(nkeys)[None, :]
        rel = (col + key0) - jnp.arange(blk)[:, None]
        b = _bias_lookup(rel_bias[:, :n_heads].astype(F32) * LOG2_E,
                         _rel_bucket(rel * plan["dil"], rel_bias.shape[0]))
        band = jnp.abs(rel) <= half
        if plan["whole"]:
            masks = [band]
        else:
            masks = [band, band & (col >= half), band & (col < blk + half)]
        tables.append(jnp.stack([jnp.where(mk[None], b, NEG_INF) for mk in masks], axis=1))
    return tables


def _dilated_attn(proj3, biases, n_heads, plans):
    b, s_len, _ = proj3.shape
    pad = max([p["half"] * p["dil"] for p in plans if not p["whole"] and p["dil"] > 1], default=0)
    pad16 = max([p["half"] for p in plans if p["dil"] == 1], default=16)
    pad16 = -(-pad16 // 16) * 16
    kern = functools.partial(_dilated_kernel, plans=plans, pad=pad, pad16=pad16)
    col = lambda off: (lambda bi, h: (bi, 0, off + h))
    return pl.pallas_call(
        kern,
        grid=(b, n_heads),
        in_specs=[
            pl.BlockSpec((1, s_len, HEAD_DIM), col(0)),
            pl.BlockSpec((1, s_len, HEAD_DIM), col(n_heads)),
            pl.BlockSpec((1, s_len, HEAD_DIM), col(2 * n_heads)),
        ] + [pl.BlockSpec((1,) + t.shape[1:], lambda bi, h: (h, 0, 0, 0)) for t in biases],
        out_specs=pl.BlockSpec((1, s_len, HEAD_DIM), lambda bi, h: (bi, 0, h)),
        out_shape=jax.ShapeDtypeStruct((b, s_len, n_heads * HEAD_DIM), F32),
        scratch_shapes=[
            pltpu.VMEM((s_len, HEAD_DIM), F32),
            pltpu.VMEM((s_len + 2 * pad, HEAD_DIM), F32),
            pltpu.VMEM((s_len + 2 * pad, HEAD_DIM), F32),
            pltpu.VMEM((s_len + 2 * pad16, HEAD_DIM), BF16),
            pltpu.VMEM((s_len + 2 * pad16, HEAD_DIM), BF16),
            pltpu.VMEM((s_len, HEAD_DIM), F32),
            pltpu.VMEM((s_len, LANES), F32),
            pltpu.VMEM((s_len, LANES), F32),
        ],
        compiler_params=pltpu.CompilerParams(
            dimension_semantics=("parallel", "parallel"), vmem_limit_bytes=VMEM_LIMIT),
        name="dilated_attn",
    )(proj3, proj3, proj3, *biases)


def _diff_attn_kernel(q_ref, k_ref, v_ref, bias_ref, lq1_ref, lk1_ref, lq2_ref, lk2_ref, g_ref,
                      o_ref, qs0_ref, qs1_ref, s0_ref, s1_ref, m_ref, acc_ref, vtx_ref, *, tq, tk, heads,
                      lambda_init):
    s_len = k_ref.shape[1]
    nq, nk = s_len // tq, s_len // tk
    qsub, ksub = tq // LANES, tk // LANES
    n_tiles = heads * nq
    tile_shift = nq.bit_length() - 1
    dn_t = (((1,), (1,)), ((), ()))
    s_refs = (s0_ref, s1_ref)
    qs_refs = (qs0_ref, qs1_ref)
    assert nq % 2 == 0 and nq & (nq - 1) == 0

    eye = (lax.broadcasted_iota(jnp.int32, (HEAD_DIM, HEAD_DIM), 0)
           == lax.broadcasted_iota(jnp.int32, (HEAD_DIM, HEAD_DIM), 1)).astype(BF16)
    for hh in range(heads):
        for c in range(nk):
            vc = v_ref[0, pl.ds(c * tk, tk), pl.ds(hh * HEAD_DIM, HEAD_DIM)]
            vtx_ref[hh, pl.ds(0, HEAD_DIM), pl.ds(c * tk, tk)] = lax.dot_general(
                eye, vc, dn_t, preferred_element_type=F32).astype(BF16)
        vtx_ref[hh, pl.ds(HEAD_DIM, ONES_ROWS), :] = jnp.ones((ONES_ROWS, s_len), BF16)
    acc_ref[...] = jnp.zeros(acc_ref.shape, F32)
    lam = (jnp.exp(jnp.sum(lq1_ref[...] * lk1_ref[...], axis=-1, keepdims=True))
           - jnp.exp(jnp.sum(lq2_ref[...] * lk2_ref[...], axis=-1, keepdims=True))
           + lambda_init)

    def split(u):
        if isinstance(u, int):
            hh, t = divmod(u, nq)
            return hh, t, pl.ds(hh * HEAD_DIM, HEAD_DIM), pl.ds(t * tq, tq)
        hh, t = u >> tile_shift, u & (nq - 1)
        return (hh, t, pl.ds(pl.multiple_of(hh * HEAD_DIM, HEAD_DIM), HEAD_DIM),
                pl.ds(pl.multiple_of(t * tq, tq), tq))

    def start_tile(u, slot):
        _, _, lanes, rows = split(u)
        q = q_ref[0, rows, lanes]
        lane = lax.broadcasted_iota(jnp.int32, (tq, HEAD_DIM), 1)
        qs_refs[slot][pl.ds(0, tq), :] = jnp.where(lane < DIFF_QK_DIM, q, 0).astype(BF16)
        qs_refs[slot][pl.ds(tq, tq), :] = jnp.where(lane >= DIFF_QK_DIM, q, 0).astype(BF16)

    def score_chunk(u, c, slot):
        hh, t, lanes, _ = split(u)
        k0 = c * tk
        s = lax.dot_general(k_ref[0, pl.ds(k0, tk), lanes], qs_refs[slot][...], dn_t,
                            preferred_element_type=F32)
        bias_rows = []
        for cc in range(ksub):
            tiles = []
            for a in range(qsub):
                diff = (c * ksub + cc) - (t * qsub + a)
                idx = jnp.minimum(jnp.maximum(diff, -DIFF_BIAS_DMAX), DIFF_BIAS_DMAX) + DIFF_BIAS_DMAX
                tiles.append(bias_ref[hh, idx])
            bias_rows.append(jnp.concatenate(tiles + tiles, axis=1))
        s = s + jnp.concatenate(bias_rows, axis=0)
        s_refs[slot][pl.ds(k0, tk), :] = s
        m_chunk = jnp.max(s.reshape(tk // 8, 8, 2 * tq), axis=0)
        m_ref[0] = m_chunk if c == 0 else jnp.maximum(m_ref[0], m_chunk)

    def finish_scores(slot):
        m_ref[1 + slot] = jnp.broadcast_to(jnp.max(m_ref[0], axis=0, keepdims=True), (8, 2 * tq))

    def value_chunk(u, c, slot):
        hh = split(u)[0]
        k0 = c * tk
        p = jnp.exp2(s_refs[slot][pl.ds(k0, tk), :] - m_ref[1 + slot, pl.ds(0, 1), :]).astype(BF16)
        acc_ref[...] += jnp.dot(vtx_ref[hh, :, pl.ds(k0, tk)], p, preferred_element_type=F32)

    def finish_tile(u):
        _, _, lanes, rows = split(u)
        acc = acc_ref[...]
        o = acc[:HEAD_DIM] / acc[HEAD_DIM:HEAD_DIM + 1]
        od = o[:, :tq] - lam * o[:, tq:]
        ms = jnp.mean(od * od, axis=0, keepdims=True)
        y = jnp.transpose(od * lax.rsqrt(ms + EPS)) * g_ref[...] * (1.0 - lambda_init)
        o_ref[0, rows, lanes] = y.astype(o_ref.dtype)
        acc_ref[...] = jnp.zeros(acc_ref.shape, F32)

    def tile_step(u, slot):
        start_tile(u, slot)
        for c in range(nk):
            score_chunk(u, c, slot)
            if c == nk - 1:
                finish_scores(slot)
            value_chunk(u - 1, c, 1 - slot)
        finish_tile(u - 1)

    start_tile(0, 0)
    for c in range(nk):
        score_chunk(0, c, 0)
    finish_scores(0)

    def pair_step(w, carry):
        tile_step(2 * w + 1, 1)
        tile_step(2 * w + 2, 0)
        return carry

    lax.fori_loop(0, n_tiles // 2 - 1, pair_step, 0)
    tile_step(n_tiles - 1, 1)
    for c in range(nk):
        value_chunk(n_tiles - 1, c, 1)
    finish_tile(n_tiles - 1)


def _diff_bias(rel_bias, n_heads_a, n_heads_b):
    t = jnp.arange(2 * DIFF_BIAS_DMAX + 1) - DIFF_BIAS_DMAX
    rel = (LANES * t[:, None, None] + jnp.arange(LANES)[None, :, None] - jnp.arange(LANES)[None, None, :])
    table = rel_bias[:, n_heads_a:n_heads_a + n_heads_b].astype(F32) * LOG2_E
    return _bias_lookup(table, _rel_bucket(rel, rel_bias.shape[0]))


def _diff_attn(proj3, bias, lams, subln_g, n_heads_a, n_heads_b, lambda_init):
    b, s_len, _ = proj3.shape
    tq = _pick(s_len, (256, 128))
    tk = _pick(s_len, (256, 128))
    heads = _pick(n_heads_b, [h for h in (8, 4, 2, 1) if h * s_len <= DIFF_HEAD_ROWS])
    width = heads * HEAD_DIM
    base = 3 * n_heads_a // heads
    assert (3 * n_heads_a) % heads == 0
    kern = functools.partial(_diff_attn_kernel, tq=tq, tk=tk, heads=heads, lambda_init=lambda_init)
    vec = lambda n: pl.BlockSpec((1, n), lambda bi, h: (0, 0))
    col = lambda off: (lambda bi, h: (bi, 0, off + h))
    groups = n_heads_b // heads
    return pl.pallas_call(
        kern,
        grid=(b, groups),
        in_specs=[
            pl.BlockSpec((1, s_len, width), col(base)),
            pl.BlockSpec((1, s_len, width), col(base + groups)),
            pl.BlockSpec((1, s_len, width), col(base + 2 * groups)),
            pl.BlockSpec((heads,) + bias.shape[1:], lambda bi, h: (h, 0, 0, 0)),
            vec(DIFF_QK_DIM), vec(DIFF_QK_DIM), vec(DIFF_QK_DIM), vec(DIFF_QK_DIM),
            vec(HEAD_DIM),
        ],
        out_specs=pl.BlockSpec((1, s_len, width), col(0)),
        out_shape=jax.ShapeDtypeStruct((b, s_len, n_heads_b * HEAD_DIM), BF16),
        scratch_shapes=[
            pltpu.VMEM((2 * tq, HEAD_DIM), BF16),
            pltpu.VMEM((2 * tq, HEAD_DIM), BF16),
            pltpu.VMEM((s_len, 2 * tq), F32),
            pltpu.VMEM((s_len, 2 * tq), F32),
            pltpu.VMEM((3, 8, 2 * tq), F32),
            pltpu.VMEM((HEAD_DIM + ONES_ROWS, 2 * tq), F32),
            pltpu.VMEM((heads, HEAD_DIM + ONES_ROWS, s_len), BF16),
        ],
        compiler_params=pltpu.CompilerParams(
            dimension_semantics=("parallel", "parallel"), vmem_limit_bytes=VMEM_LIMIT),
        name="diff_attn",
    )(proj3, proj3, proj3, bias, *lams, subln_g)


def _out_proj_kernel(oa_ref, ob_ref, x_ref, wa_ref, wb_ref, ga_ref, gp_ref, o_ref):
    oa = _rms(oa_ref[...], ga_ref[...]).astype(BF16)
    y = jnp.dot(oa, wa_ref[...], preferred_element_type=F32)
    y = y + jnp.dot(ob_ref[...], wb_ref[...], preferred_element_type=F32)
    o_ref[...] = x_ref[...] + _rms(y, gp_ref[...])


def _out_proj(oa2, ob2, x2, wa, wb, ga, gp):
    t, d = x2.shape
    wa_w, wb_w = wa.shape[0], wb.shape[0]
    tm = _pick(t, (512, 256, 128))
    const = lambda i: (0, 0)
    row = lambda i: (i, 0)
    return pl.pallas_call(
        _out_proj_kernel,
        grid=(t // tm,),
        in_specs=[
            pl.BlockSpec((tm, wa_w), row),
            pl.BlockSpec((tm, wb_w), row),
            pl.BlockSpec((tm, d), row),
            pl.BlockSpec((wa_w, d), const),
            pl.BlockSpec((wb_w, d), const),
            pl.BlockSpec((1, wa_w), const),
            pl.BlockSpec((1, d), const),
        ],
        out_specs=pl.BlockSpec((tm, d), row),
        out_shape=jax.ShapeDtypeStruct((t, d), F32),
        compiler_params=pltpu.CompilerParams(
            dimension_semantics=("parallel",), vmem_limit_bytes=VMEM_LIMIT),
        name="out_proj",
    )(oa2, ob2, x2, wa, wb, ga, gp)


def _ffn_kernel(x_ref, gpre_ref, wg_ref, wu_ref, wd_ref, gpost_ref, o_ref, h_ref, acc_ref):
    j = pl.program_id(1)

    @pl.when(j == 0)
    def _():
        h_ref[...] = _rms(x_ref[...], gpre_ref[...]).astype(BF16)
        acc_ref[...] = jnp.zeros(acc_ref.shape, F32)

    h = h_ref[...]
    gate = jnp.dot(h, wg_ref[...], preferred_element_type=F32)
    up = jnp.dot(h, wu_ref[...], preferred_element_type=F32)
    f = gate * (1.0 / (1.0 + jnp.exp(-gate))) * up
    acc_ref[...] += jnp.dot(f.astype(BF16), wd_ref[...], preferred_element_type=F32)

    @pl.when(j == pl.num_programs(1) - 1)
    def _():
        o_ref[...] = x_ref[...] + _rms(acc_ref[...], gpost_ref[...])


def _ffn(x2, gpre, wg, wu, wd, gpost):
    t, d = x2.shape
    f = wg.shape[1]
    tm = _pick(t, (512, 256, 128))
    tf = _pick(f, (512, 256, 128))
    return pl.pallas_call(
        _ffn_kernel,
        grid=(t // tm, f // tf),
        in_specs=[
            pl.BlockSpec((tm, d), lambda i, j: (i, 0)),
            pl.BlockSpec((1, d), lambda i, j: (0, 0)),
            pl.BlockSpec((d, tf), lambda i, j: (0, j)),
            pl.BlockSpec((d, tf), lambda i, j: (0, j)),
            pl.BlockSpec((tf, d), lambda i, j: (j, 0)),
            pl.BlockSpec((1, d), lambda i, j: (0, 0)),
        ],
        out_specs=pl.BlockSpec((tm, d), lambda i, j: (i, 0)),
        out_shape=jax.ShapeDtypeStruct((t, d), F32),
        scratch_shapes=[pltpu.VMEM((tm, d), BF16), pltpu.VMEM((tm, d), F32)],
        compiler_params=pltpu.CompilerParams(
            dimension_semantics=("parallel", "arbitrary"), vmem_limit_bytes=VMEM_LIMIT),
        name="ffn",
    )(x2, gpre, wg, wu, wd, gpost)


def _encoder_layer(x, layer, p, bias_a, bias_b):
    b, s_len, d = x.shape
    width = p["w_in"].shape[1] // 6
    n_heads = width // HEAD_DIM
    lambda_init = 0.8 - 0.6 * math.exp(-0.3 * layer)
    row = lambda v: v.reshape(1, -1)

    x2 = x.reshape(b * s_len, d)
    col_scale = (jnp.ones((1, 6 * width), F32).at[:, :width].set(DILATED_Q_SCALE)
                 .at[:, 3 * width:4 * width].set(DIFF_Q_SCALE))
    proj3 = _in_proj(x2, row(p["pre_mix_g"]), p["w_in"], col_scale).reshape(b, s_len, 6 * width)
    oa = _dilated_attn(proj3, bias_a[s_len], n_heads, _dilated_plan(s_len))
    lams = [row(p[k]) for k in ("lambda_q1", "lambda_k1", "lambda_q2", "lambda_k2")]
    ob = _diff_attn(proj3, bias_b, lams, row(p["subln_g"]), n_heads, n_heads, lambda_init)
    x1 = _out_proj(oa.reshape(b * s_len, width), ob.reshape(b * s_len, width), x2,
                   p["w_out"][:width], p["w_out"][width:], row(p["norm_a_g"]), row(p["post_mix_g"]))
    y = _ffn(x1, row(p["pre_ffn_g"]), p["w_gate"], p["w_up"], p["w_down"], row(p["post_ffn_g"]))
    return y.reshape(b, s_len, d)


def kernel(x_prompt, x_sample, pre_mix_g, post_mix_g, pre_ffn_g, post_ffn_g, w_in, w_out, norm_a_g,
           lambda_q1, lambda_k1, lambda_q2, lambda_k2, subln_g, w_gate, w_up, w_down, rel_bias):
    depth = w_in.shape[0]
    n_heads = w_in.shape[2] // 6 // HEAD_DIM
    bias_a = {x.shape[1]: _dilated_bias(rel_bias, n_heads, _dilated_plan(x.shape[1]))
              for x in (x_prompt, x_sample)}
    bias_b = _diff_bias(rel_bias, n_heads, n_heads)
    layers = []
    for layer in range(depth):
        layers.append(dict(
            pre_mix_g=pre_mix_g[layer], post_mix_g=post_mix_g[layer], pre_ffn_g=pre_ffn_g[layer],
            post_ffn_g=post_ffn_g[layer], norm_a_g=norm_a_g[layer], subln_g=subln_g[layer],
            lambda_q1=lambda_q1[layer], lambda_k1=lambda_k1[layer],
            lambda_q2=lambda_q2[layer], lambda_k2=lambda_k2[layer],
            w_in=w_in[layer].astype(BF16), w_out=w_out[layer].astype(BF16),
            w_gate=w_gate[layer].astype(BF16), w_up=w_up[layer].astype(BF16),
            w_down=w_down[layer].astype(BF16)))
    outs = []
    for x in (x_prompt, x_sample):
        for layer, p in enumerate(layers):
            x = _encoder_layer(x, layer, p, bias_a, bias_b)
        outs.append(x)
    return tuple(outs)
```

```python
import functools
import math

import jax
import jax.numpy as jnp
from jax import lax
from jax.experimental import pallas as pl
from jax.experimental.pallas import tpu as pltpu

F32 = jnp.float32
BF16 = jnp.bfloat16

EPS = 1e-6
NEG_INF = -1e30
HEAD_DIM = 128
DIFF_QK_DIM = 64
DILATED_CONFIGS = ((128, 1), (512, 4), (2048, 16))
MAX_DISTANCE = 1024
LANES = 128
VMEM_LIMIT = 56 * 1024 * 1024

DIFF_BIAS_DMAX = -(-(MAX_DISTANCE + LANES - 1) // LANES)
LOG2_E = math.log2(math.e)
DILATED_Q_SCALE = LOG2_E / math.sqrt(HEAD_DIM)
DIFF_Q_SCALE = LOG2_E / math.sqrt(DIFF_QK_DIM)
ONES_ROWS = 16
DIFF_HEAD_ROWS = 8192


def _pick(n, candidates):
    for c in candidates:
        if n % c == 0:
            return c
    raise ValueError(f"no tile in {candidates} divides {n}")


def _rms(x, g):
    ms = jnp.mean(x * x, axis=-1, keepdims=True)
    return x * lax.rsqrt(ms + EPS) * g


def _rel_bucket(rel, n_buckets):
    half = n_buckets // 2
    max_exact = half // 2
    n = jnp.abs(rel)
    nf = jnp.maximum(n, 1).astype(F32)
    large = max_exact + (jnp.log(nf / max_exact) / math.log(MAX_DISTANCE / max_exact)
                         * (half - max_exact)).astype(jnp.int32)
    large = jnp.minimum(large, half - 1)
    return jnp.where(rel > 0, half, 0) + jnp.where(n < max_exact, n, large)


def _bias_lookup(table, bucket):
    cols = table.astype(F32).T.reshape((table.shape[1],) + (1,) * bucket.ndim + (table.shape[0],))
    out = jnp.zeros((table.shape[1],) + bucket.shape, F32)
    for b in range(table.shape[0]):
        out = jnp.where(bucket[None] == b, cols[..., b], out)
    return out


def _in_proj_kernel(x_ref, g_ref, w_ref, c_ref, o_ref, h_ref):
    @pl.when(pl.program_id(1) == 0)
    def _():
        h_ref[...] = _rms(x_ref[...], g_ref[...]).astype(BF16)

    y = jnp.dot(h_ref[...], w_ref[...], preferred_element_type=F32)
    o_ref[...] = (y * c_ref[...]).astype(o_ref.dtype)


def _in_proj(x2, g, w, col_scale):
    t, d = x2.shape
    n = w.shape[1]
    tm = _pick(t, (1024, 512, 256, 128))
    tn = _pick(n, (1024, 768, 512, 384, 256, 128))
    return pl.pallas_call(
        _in_proj_kernel,
        grid=(t // tm, n // tn),
        in_specs=[
            pl.BlockSpec((tm, d), lambda i, j: (i, 0)),
            pl.BlockSpec((1, d), lambda i, j: (0, 0)),
            pl.BlockSpec((d, tn), lambda i, j: (0, j)),
            pl.BlockSpec((1, tn), lambda i, j: (0, j)),
        ],
        out_specs=pl.BlockSpec((tm, tn), lambda i, j: (i, j)),
        out_shape=jax.ShapeDtypeStruct((t, n), BF16),
        scratch_shapes=[pltpu.VMEM((tm, d), BF16)],
        compiler_params=pltpu.CompilerParams(
            dimension_semantics=("parallel", "arbitrary"), vmem_limit_bytes=VMEM_LIMIT),
        name="in_proj",
    )(x2, g, w, col_scale)


def _dilated_plan(s_len):
    plans = []
    for window, dil in sorted(DILATED_CONFIGS, key=lambda c: -c[1]):
        half = window // (2 * dil)
        sub_len = s_len // dil
        assert s_len % dil == 0 and dil & (dil - 1) == 0
        if sub_len <= 2 * LANES:
            blk, nkeys, nblocks, whole = sub_len, sub_len, dil, True
        else:
            blk, nkeys, nblocks, whole = 2 * half, 4 * half, s_len // (2 * half), False
            assert sub_len % blk == 0 and sub_len // blk >= 2
        assert blk % 8 == 0 and nkeys % LANES == 0
        group = _pick(nblocks, [g for g in (8, 4, 2, 1) if g * blk * nkeys <= 8 * LANES * 2 * LANES])
        plans.append(dict(dil=dil, half=half, sub_len=sub_len, blk=blk, nkeys=nkeys, nblocks=nblocks,
                          whole=whole, group=group))
    return plans


def _dilated_kernel(q_ref, k_ref, v_ref, *rest, plans, pad, pad16):
    bias_refs = rest[:len(plans)]
    o_ref, qf, kf, vf, k16, v16, acc, m_st, l_st = rest[len(plans):]
    s_len = q_ref.shape[1]

    U32 = jnp.uint32
    qf[...] = pltpu.bitcast(q_ref[0], U32)
    for buf, src in ((kf, k_ref), (vf, v_ref)):
        if pad:
            buf[pl.ds(0, pad // 2), :] = jnp.zeros((pad // 2, HEAD_DIM), U32)
            buf[pl.ds((pad + s_len) // 2, pad // 2), :] = jnp.zeros((pad // 2, HEAD_DIM), U32)
        buf[pl.ds(pad // 2, s_len // 2), :] = pltpu.bitcast(src[0], U32)

    def word_rows(start, size, stride):
        return pl.ds(start, size) if stride == 1 else pl.ds(start, size, stride=stride)

    def split_words(w):
        low = pltpu.bitcast(w << 16, F32).astype(BF16)
        high = pltpu.bitcast(w & U32(0xFFFF0000), F32).astype(BF16)
        return low, high
    for buf, src in ((k16, k_ref), (v16, v_ref)):
        buf[pl.ds(0, pad16), :] = jnp.zeros((pad16, HEAD_DIM), BF16)
        buf[pl.ds(pad16 + s_len, pad16), :] = jnp.zeros((pad16, HEAD_DIM), BF16)
        buf[pl.ds(pad16, s_len), :] = src[0]

    for ci, plan in enumerate(plans):
        dil, half, blk, nkeys = plan["dil"], plan["half"], plan["blk"], plan["nkeys"]
        whole, group = plan["whole"], plan["group"]
        shift = dil.bit_length() - 1
        nblk_phase = plan["sub_len"] // blk
        bias_ref = bias_refs[ci]

        def scores(it, ci=ci, dil=dil, half=half, blk=blk, nkeys=nkeys, whole=whole, shift=shift,
                   nblk_phase=nblk_phase, bias_ref=bias_ref):
            if whole:
                phase, sub0, key0, variant = it, 0, 0, 0
            else:
                phase, n = it & (dil - 1), it >> shift
                sub0 = n * blk
                key0 = sub0 - half
                variant = jnp.where(n == 0, 1, jnp.where(n == nblk_phase - 1, 2, 0))
            if dil == 1:
                q_rows = pl.ds(pl.multiple_of(sub0, blk), blk)
                kv_rows = pl.ds(pl.multiple_of(pad16 + key0, half), nkeys)
                operands = [(q_rows, q_ref[0, q_rows, :], k16[kv_rows, :], v16[kv_rows, :])]
            else:
                hd = dil // 2
                qs = split_words(qf[word_rows((phase >> 1) + hd * sub0, blk, hd), :])
                kv_words = word_rows(pad // 2 + (phase >> 1) + hd * key0, nkeys, hd)
                ks, vs = split_words(kf[kv_words, :]), split_words(vf[kv_words, :])
                operands = [(pl.ds(phase + b + dil * sub0, blk, stride=dil), qs[b], ks[b], vs[b])
                            for b in range(2)]
            out = []
            for q_rows, qb, kb, vb in operands:
                old = (m_st[q_rows, :], l_st[q_rows, :], acc[q_rows, :]) if ci > 0 else None
                s = lax.dot_general(qb, kb, (((1,), (1,)), ((), ())), preferred_element_type=F32)
                out.append((q_rows, vb, old, s + bias_ref[0, variant]))
            return out

        def softmax(old, s, ci=ci, blk=blk, nkeys=nkeys):
            m_cur = jnp.max(s, axis=1, keepdims=True)
            if ci == 0:
                m_new = jnp.broadcast_to(m_cur, (blk, LANES))
            else:
                m_new = jnp.maximum(old[0], m_cur)
            p = jnp.exp2(s - jnp.concatenate([m_new] * (nkeys // LANES), axis=1))
            return m_new, jnp.sum(p, axis=1, keepdims=True), p.astype(BF16)

        def values(old, vb, m_new, l_new, p, ci=ci, blk=blk):
            a_new = jnp.dot(p, vb, preferred_element_type=F32)
            if ci == 0:
                return jnp.broadcast_to(l_new, (blk, LANES)), a_new
            alpha = jnp.exp2(old[0] - m_new)
            return alpha * old[1] + l_new, alpha * old[2] + a_new

        per_call = 1 if dil == 1 else 2
        assert group % per_call == 0 and dil % 2 == per_call % 2

        def body(j, carry, scores=scores, softmax=softmax, values=values, group=group, per_call=per_call):
            staged = [blk_ops for g in range(0, group, per_call) for blk_ops in scores(j * group + g)]
            probs = [softmax(old, s) for _, _, old, s in staged]
            for (q_rows, vb, old, _), (m_new, l_new, p) in zip(staged, probs):
                l_new, a_new = values(old, vb, m_new, l_new, p)
                m_st[q_rows, :] = m_new
                l_st[q_rows, :] = l_new
                acc[q_rows, :] = a_new
            return carry

        lax.fori_loop(0, plan["nblocks"] // group, body, 0)

    o_ref[0] = (acc[...] / l_st[...]).astype(o_ref.dtype)


def _dilated_bias(rel_bias, n_heads, plans):
    tables = []
    for plan in plans:
        half, blk, nkeys = plan["half"], plan["blk"], plan["nkeys"]
        key0 = 0 if plan["whole"] else -half
        col = jnp.arange(nkeys)[None, :]
        rel = (col + key0) - jnp.arange(blk)[:, None]
        b = _bias_lookup(rel_bias[:, :n_heads].astype(F32) * LOG2_E,
                         _rel_bucket(rel * plan["dil"], rel_bias.shape[0]))
        band = jnp.abs(rel) <= half
        if plan["whole"]:
            masks = [band]
        else:
            masks = [band, band & (col >= half), band & (col < blk + half)]
        tables.append(jnp.stack([jnp.where(mk[None], b, NEG_INF) for mk in masks], axis=1))
    return tables


def _dilated_attn(proj3, biases, n_heads, plans):
    b, s_len, _ = proj3.shape
    pad = max([p["half"] * p["dil"] for p in plans if not p["whole"] and p["dil"] > 1], default=0)
    pad16 = max([p["half"] for p in plans if p["dil"] == 1], default=16)
    pad16 = -(-pad16 // 16) * 16
    kern = functools.partial(_dilated_kernel, plans=plans, pad=pad, pad16=pad16)
    col = lambda off: (lambda bi, h: (bi, 0, off + h))
    return pl.pallas_call(
        kern,
        grid=(b, n_heads),
        in_specs=[
            pl.BlockSpec((1, s_len, HEAD_DIM), col(0)),
            pl.BlockSpec((1, s_len, HEAD_DIM), col(n_heads)),
            pl.BlockSpec((1, s_len, HEAD_DIM), col(2 * n_heads)),
        ] + [pl.BlockSpec((1,) + t.shape[1:], lambda bi, h: (h, 0, 0, 0)) for t in biases],
        out_specs=pl.BlockSpec((1, s_len, HEAD_DIM), lambda bi, h: (bi, 0, h)),
        out_shape=jax.ShapeDtypeStruct((b, s_len, n_heads * HEAD_DIM), F32),
        scratch_shapes=[
            pltpu.VMEM((s_len // 2, HEAD_DIM), jnp.uint32),
            pltpu.VMEM((s_len // 2 + pad, HEAD_DIM), jnp.uint32),
            pltpu.VMEM((s_len // 2 + pad, HEAD_DIM), jnp.uint32),
            pltpu.VMEM((s_len + 2 * pad16, HEAD_DIM), BF16),
            pltpu.VMEM((s_len + 2 * pad16, HEAD_DIM), BF16),
            pltpu.VMEM((s_len, HEAD_DIM), F32),
            pltpu.VMEM((s_len, LANES), F32),
            pltpu.VMEM((s_len, LANES), F32),
        ],
        compiler_params=pltpu.CompilerParams(
            dimension_semantics=("parallel", "parallel"), vmem_limit_bytes=VMEM_LIMIT),
        name="dilated_attn",
    )(proj3, proj3, proj3, *biases)


def _diff_attn_kernel(q_ref, k_ref, v_ref, bias_ref, lq1_ref, lk1_ref, lq2_ref, lk2_ref, g_ref,
                      o_ref, qs0_ref, qs1_ref, s0_ref, s1_ref, m_ref, acc_ref, vtx_ref, *, tq, tk, heads,
                      lambda_init):
    s_len = k_ref.shape[1]
    nq, nk = s_len // tq, s_len // tk
    qsub, ksub = tq // LANES, tk // LANES
    n_tiles = heads * nq
    tile_shift = nq.bit_length() - 1
    dn_t = (((1,), (1,)), ((), ()))
    s_refs = (s0_ref, s1_ref)
    qs_refs = (qs0_ref, qs1_ref)
    assert nq % 2 == 0 and nq & (nq - 1) == 0

    eye = (lax.broadcasted_iota(jnp.int32, (HEAD_DIM, HEAD_DIM), 0)
           == lax.broadcasted_iota(jnp.int32, (HEAD_DIM, HEAD_DIM), 1)).astype(BF16)
    for hh in range(heads):
        for c in range(nk):
            vc = v_ref[0, pl.ds(c * tk, tk), pl.ds(hh * HEAD_DIM, HEAD_DIM)]
            vtx_ref[hh, pl.ds(0, HEAD_DIM), pl.ds(c * tk, tk)] = lax.dot_general(
                eye, vc, dn_t, preferred_element_type=F32).astype(BF16)
        vtx_ref[hh, pl.ds(HEAD_DIM, ONES_ROWS), :] = jnp.ones((ONES_ROWS, s_len), BF16)
    acc_ref[...] = jnp.zeros(acc_ref.shape, F32)
    lam = (jnp.exp(jnp.sum(lq1_ref[...] * lk1_ref[...], axis=-1, keepdims=True))
           - jnp.exp(jnp.sum(lq2_ref[...] * lk2_ref[...], axis=-1, keepdims=True))
           + lambda_init)

    def split(u):
        if isinstance(u, int):
            hh, t = divmod(u, nq)
            return hh, t, pl.ds(hh * HEAD_DIM, HEAD_DIM), pl.ds(t * tq, tq)
        hh, t = u >> tile_shift, u & (nq - 1)
        return (hh, t, pl.ds(pl.multiple_of(hh * HEAD_DIM, HEAD_DIM), HEAD_DIM),
                pl.ds(pl.multiple_of(t * tq, tq), tq))

    def start_tile(u, slot):
        _, _, lanes, rows = split(u)
        q = q_ref[0, rows, lanes]
        lane = lax.broadcasted_iota(jnp.int32, (tq, HEAD_DIM), 1)
        qs_refs[slot][pl.ds(0, tq), :] = jnp.where(lane < DIFF_QK_DIM, q, 0).astype(BF16)
        qs_refs[slot][pl.ds(tq, tq), :] = jnp.where(lane >= DIFF_QK_DIM, q, 0).astype(BF16)

    def score_chunk(u, c, slot):
        hh, t, lanes, _ = split(u)
        k0 = c * tk
        s = lax.dot_general(k_ref[0, pl.ds(k0, tk), lanes], qs_refs[slot][...], dn_t,
                            preferred_element_type=F32)
        bias_rows = []
        for cc in range(ksub):
            tiles = []
            for a in range(qsub):
                diff = (c * ksub + cc) - (t * qsub + a)
                idx = jnp.minimum(jnp.maximum(diff, -DIFF_BIAS_DMAX), DIFF_BIAS_DMAX) + DIFF_BIAS_DMAX
                tiles.append(bias_ref[hh, idx])
            bias_rows.append(jnp.concatenate(tiles + tiles, axis=1))
        s = s + jnp.concatenate(bias_rows, axis=0)
        s_refs[slot][pl.ds(k0, tk), :] = s
        m_chunk = jnp.max(s.reshape(tk // 8, 8, 2 * tq), axis=0)
        m_ref[0] = m_chunk if c == 0 else jnp.maximum(m_ref[0], m_chunk)

    def finish_scores(slot):
        m_ref[1 + slot] = jnp.broadcast_to(jnp.max(m_ref[0], axis=0, keepdims=True), (8, 2 * tq))

    def value_chunk(u, c, slot):
        hh = split(u)[0]
        k0 = c * tk
        p = jnp.exp2(s_refs[slot][pl.ds(k0, tk), :] - m_ref[1 + slot, pl.ds(0, 1), :]).astype(BF16)
        acc_ref[...] += jnp.dot(vtx_ref[hh, :, pl.ds(k0, tk)], p, preferred_element_type=F32)

    def finish_tile(u):
        _, _, lanes, rows = split(u)
        acc = acc_ref[...]
        o = acc[:HEAD_DIM] / acc[HEAD_DIM:HEAD_DIM + 1]
        od = o[:, :tq] - lam * o[:, tq:]
        ms = jnp.mean(od * od, axis=0, keepdims=True)
        y = jnp.transpose(od * lax.rsqrt(ms + EPS)) * g_ref[...] * (1.0 - lambda_init)
        o_ref[0, rows, lanes] = y.astype(o_ref.dtype)
        acc_ref[...] = jnp.zeros(acc_ref.shape, F32)

    def tile_step(u, slot):
        start_tile(u, slot)
        for c in range(nk):
            score_chunk(u, c, slot)
            if c == nk - 1:
                finish_scores(slot)
            value_chunk(u - 1, c, 1 - slot)
        finish_tile(u - 1)

    start_tile(0, 0)
    for c in range(nk):
        score_chunk(0, c, 0)
    finish_scores(0)

    def pair_step(w, carry):
        tile_step(2 * w + 1, 1)
        tile_step(2 * w + 2, 0)
        return carry

    lax.fori_loop(0, n_tiles // 2 - 1, pair_step, 0)
    tile_step(n_tiles - 1, 1)
    for c in range(nk):
        value_chunk(n_tiles - 1, c, 1)
    finish_tile(n_tiles - 1)


def _diff_bias(rel_bias, n_heads_a, n_heads_b):
    t = jnp.arange(2 * DIFF_BIAS_DMAX + 1) - DIFF_BIAS_DMAX
    rel = (LANES * t[:, None, None] + jnp.arange(LANES)[None, :, None] - jnp.arange(LANES)[None, None, :])
    table = rel_bias[:, n_heads_a:n_heads_a + n_heads_b].astype(F32) * LOG2_E
    return _bias_lookup(table, _rel_bucket(rel, rel_bias.shape[0]))


def _diff_attn(proj3, bias, lams, subln_g, n_heads_a, n_heads_b, lambda_init):
    b, s_len, _ = proj3.shape
    tq = _pick(s_len, (256, 128))
    tk = _pick(s_len, (256, 128))
    heads = _pick(n_heads_b, [h for h in (8, 4, 2, 1) if h * s_len <= DIFF_HEAD_ROWS])
    width = heads * HEAD_DIM
    base = 3 * n_heads_a // heads
    assert (3 * n_heads_a) % heads == 0
    kern = functools.partial(_diff_attn_kernel, tq=tq, tk=tk, heads=heads, lambda_init=lambda_init)
    vec = lambda n: pl.BlockSpec((1, n), lambda bi, h: (0, 0))
    col = lambda off: (lambda bi, h: (bi, 0, off + h))
    groups = n_heads_b // heads
    return pl.pallas_call(
        kern,
        grid=(b, groups),
        in_specs=[
            pl.BlockSpec((1, s_len, width), col(base)),
            pl.BlockSpec((1, s_len, width), col(base + groups)),
            pl.BlockSpec((1, s_len, width), col(base + 2 * groups)),
            pl.BlockSpec((heads,) + bias.shape[1:], lambda bi, h: (h, 0, 0, 0)),
            vec(DIFF_QK_DIM), vec(DIFF_QK_DIM), vec(DIFF_QK_DIM), vec(DIFF_QK_DIM),
            vec(HEAD_DIM),
        ],
        out_specs=pl.BlockSpec((1, s_len, width), col(0)),
        out_shape=jax.ShapeDtypeStruct((b, s_len, n_heads_b * HEAD_DIM), BF16),
        scratch_shapes=[
            pltpu.VMEM((2 * tq, HEAD_DIM), BF16),
            pltpu.VMEM((2 * tq, HEAD_DIM), BF16),
            pltpu.VMEM((s_len, 2 * tq), F32),
            pltpu.VMEM((s_len, 2 * tq), F32),
            pltpu.VMEM((3, 8, 2 * tq), F32),
            pltpu.VMEM((HEAD_DIM + ONES_ROWS, 2 * tq), F32),
            pltpu.VMEM((heads, HEAD_DIM + ONES_ROWS, s_len), BF16),
        ],
        compiler_params=pltpu.CompilerParams(
            dimension_semantics=("parallel", "parallel"), vmem_limit_bytes=VMEM_LIMIT),
        name="diff_attn",
    )(proj3, proj3, proj3, bias, *lams, subln_g)


def _out_proj_kernel(oa_ref, ob_ref, x_ref, wa_ref, wb_ref, ga_ref, gp_ref, o_ref):
    oa = _rms(oa_ref[...], ga_ref[...]).astype(BF16)
    y = jnp.dot(oa, wa_ref[...], preferred_element_type=F32)
    y = y + jnp.dot(ob_ref[...], wb_ref[...], preferred_element_type=F32)
    o_ref[...] = x_ref[...] + _rms(y, gp_ref[...])


def _out_proj(oa2, ob2, x2, wa, wb, ga, gp):
    t, d = x2.shape
    wa_w, wb_w = wa.shape[0], wb.shape[0]
    tm = _pick(t, (512, 256, 128))
    const = lambda i: (0, 0)
    row = lambda i: (i, 0)
    return pl.pallas_call(
        _out_proj_kernel,
        grid=(t // tm,),
        in_specs=[
            pl.BlockSpec((tm, wa_w), row),
            pl.BlockSpec((tm, wb_w), row),
            pl.BlockSpec((tm, d), row),
            pl.BlockSpec((wa_w, d), const),
            pl.BlockSpec((wb_w, d), const),
            pl.BlockSpec((1, wa_w), const),
            pl.BlockSpec((1, d), const),
        ],
        out_specs=pl.BlockSpec((tm, d), row),
        out_shape=jax.ShapeDtypeStruct((t, d), F32),
        compiler_params=pltpu.CompilerParams(
            dimension_semantics=("parallel",), vmem_limit_bytes=VMEM_LIMIT),
        name="out_proj",
    )(oa2, ob2, x2, wa, wb, ga, gp)


def _ffn_kernel(x_ref, gpre_ref, wg_ref, wu_ref, wd_ref, gpost_ref, o_ref, h_ref, acc_ref):
    j = pl.program_id(1)

    @pl.when(j == 0)
    def _():
        h_ref[...] = _rms(x_ref[...], gpre_ref[...]).astype(BF16)
        acc_ref[...] = jnp.zeros(acc_ref.shape, F32)

    h = h_ref[...]
    gate = jnp.dot(h, wg_ref[...], preferred_element_type=F32)
    up = jnp.dot(h, wu_ref[...], preferred_element_type=F32)
    f = gate * (1.0 / (1.0 + jnp.exp(-gate))) * up
    acc_ref[...] += jnp.dot(f.astype(BF16), wd_ref[...], preferred_element_type=F32)

    @pl.when(j == pl.num_programs(1) - 1)
    def _():
        o_ref[...] = x_ref[...] + _rms(acc_ref[...], gpost_ref[...])


def _ffn(x2, gpre, wg, wu, wd, gpost):
    t, d = x2.shape
    f = wg.shape[1]
    tm = _pick(t, (512, 256, 128))
    tf = _pick(f, (512, 256, 128))
    return pl.pallas_call(
        _ffn_kernel,
        grid=(t // tm, f // tf),
        in_specs=[
            pl.BlockSpec((tm, d), lambda i, j: (i, 0)),
            pl.BlockSpec((1, d), lambda i, j: (0, 0)),
            pl.BlockSpec((d, tf), lambda i, j: (0, j)),
            pl.BlockSpec((d, tf), lambda i, j: (0, j)),
            pl.BlockSpec((tf, d), lambda i, j: (j, 0)),
            pl.BlockSpec((1, d), lambda i, j: (0, 0)),
        ],
        out_specs=pl.BlockSpec((tm, d), lambda i, j: (i, 0)),
        out_shape=jax.ShapeDtypeStruct((t, d), F32),
        scratch_shapes=[pltpu.VMEM((tm, d), BF16), pltpu.VMEM((tm, d), F32)],
        compiler_params=pltpu.CompilerParams(
            dimension_semantics=("parallel", "arbitrary"), vmem_limit_bytes=VMEM_LIMIT),
        name="ffn",
    )(x2, gpre, wg, wu, wd, gpost)


def _encoder_layer(x, layer, p, bias_a, bias_b):
    b, s_len, d = x.shape
    width = p["w_in"].shape[1] // 6
    n_heads = width // HEAD_DIM
    lambda_init = 0.8 - 0.6 * math.exp(-0.3 * layer)
    row = lambda v: v.reshape(1, -1)

    x2 = x.reshape(b * s_len, d)
    col_scale = (jnp.ones((1, 6 * width), F32).at[:, :width].set(DILATED_Q_SCALE)
                 .at[:, 3 * width:4 * width].set(DIFF_Q_SCALE))
    proj3 = _in_proj(x2, row(p["pre_mix_g"]), p["w_in"], col_scale).reshape(b, s_len, 6 * width)
    oa = _dilated_attn(proj3, bias_a[s_len], n_heads, _dilated_plan(s_len))
    lams = [row(p[k]) for k in ("lambda_q1", "lambda_k1", "lambda_q2", "lambda_k2")]
    ob = _diff_attn(proj3, bias_b, lams, row(p["subln_g"]), n_heads, n_heads, lambda_init)
    x1 = _out_proj(oa.reshape(b * s_len, width), ob.reshape(b * s_len, width), x2,
                   p["w_out"][:width], p["w_out"][width:], row(p["norm_a_g"]), row(p["post_mix_g"]))
    y = _ffn(x1, row(p["pre_ffn_g"]), p["w_gate"], p["w_up"], p["w_down"], row(p["post_ffn_g"]))
    return y.reshape(b, s_len, d)


def kernel(x_prompt, x_sample, pre_mix_g, post_mix_g, pre_ffn_g, post_ffn_g, w_in, w_out, norm_a_g,
           lambda_q1, lambda_k1, lambda_q2, lambda_k2, subln_g, w_gate, w_up, w_down, rel_bias):
    depth = w_in.shape[0]
    n_heads = w_in.shape[2] // 6 // HEAD_DIM
    bias_a = {x.shape[1]: _dilated_bias(rel_bias, n_heads, _dilated_plan(x.shape[1]))
              for x in (x_prompt, x_sample)}
    bias_b = _diff_bias(rel_bias, n_heads, n_heads)
    layers = []
    for layer in range(depth):
        layers.append(dict(
            pre_mix_g=pre_mix_g[layer], post_mix_g=post_mix_g[layer], pre_ffn_g=pre_ffn_g[layer],
            post_ffn_g=post_ffn_g[layer], norm_a_g=norm_a_g[layer], subln_g=subln_g[layer],
            lambda_q1=lambda_q1[layer], lambda_k1=lambda_k1[layer],
            lambda_q2=lambda_q2[layer], lambda_k2=lambda_k2[layer],
            w_in=w_in[layer].astype(BF16), w_out=w_out[layer].astype(BF16),
            w_gate=w_gate[layer].astype(BF16), w_up=w_up[layer].astype(BF16),
            w_down=w_down[layer].astype(BF16)))
    outs = []
    for x in (x_prompt, x_sample):
        for layer, p in enumerate(layers):
            x = _encoder_layer(x, layer, p, bias_a, bias_b)
        outs.append(x)
    return tuple(outs)
```

```python
import functools
import math

import jax
import jax.numpy as jnp
from jax import lax
from jax.experimental import pallas as pl
from jax.experimental.pallas import tpu as pltpu

F32 = jnp.float32
BF16 = jnp.bfloat16

EPS = 1e-6
NEG_INF = -1e30
HEAD_DIM = 128
DIFF_QK_DIM = 64
DILATED_CONFIGS = ((128, 1), (512, 4), (2048, 16))
MAX_DISTANCE = 1024
LANES = 128
VMEM_LIMIT = 56 * 1024 * 1024

DIFF_BIAS_DMAX = -(-(MAX_DISTANCE + LANES - 1) // LANES)
LOG2_E = math.log2(math.e)
DILATED_Q_SCALE = LOG2_E / math.sqrt(HEAD_DIM)
DIFF_Q_SCALE = LOG2_E / math.sqrt(DIFF_QK_DIM)
ONES_ROWS = 16
DIFF_HEAD_ROWS = 8192


def _pick(n, candidates):
    for c in candidates:
        if n % c == 0:
            return c
    raise ValueError(f"no tile in {candidates} divides {n}")


def _rms(x, g):
    ms = jnp.mean(x * x, axis=-1, keepdims=True)
    return x * lax.rsqrt(ms + EPS) * g


def _rel_bucket(rel, n_buckets):
    half = n_buckets // 2
    max_exact = half // 2
    n = jnp.abs(rel)
    nf = jnp.maximum(n, 1).astype(F32)
    large = max_exact + (jnp.log(nf / max_exact) / math.log(MAX_DISTANCE / max_exact)
                         * (half - max_exact)).astype(jnp.int32)
    large = jnp.minimum(large, half - 1)
    return jnp.where(rel > 0, half, 0) + jnp.where(n < max_exact, n, large)


def _bias_lookup(table, bucket):
    cols = table.astype(F32).T.reshape((table.shape[1],) + (1,) * bucket.ndim + (table.shape[0],))
    out = jnp.zeros((table.shape[1],) + bucket.shape, F32)
    for b in range(table.shape[0]):
        out = jnp.where(bucket[None] == b, cols[..., b], out)
    return out


def _in_proj_kernel(x_ref, g_ref, w_ref, c_ref, o_ref, h_ref):
    @pl.when(pl.program_id(1) == 0)
    def _():
        h_ref[...] = _rms(x_ref[...], g_ref[...]).astype(BF16)

    y = jnp.dot(h_ref[...], w_ref[...], preferred_element_type=F32)
    o_ref[...] = (y * c_ref[...]).astype(o_ref.dtype)


def _in_proj(x2, g, w, col_scale):
    t, d = x2.shape
    n = w.shape[1]
    tm = _pick(t, (1024, 512, 256, 128))
    tn = _pick(n, (2048, 1024, 768, 512, 384, 256, 128))
    return pl.pallas_call(
        _in_proj_kernel,
        grid=(t // tm, n // tn),
        in_specs=[
            pl.BlockSpec((tm, d), lambda i, j: (i, 0)),
            pl.BlockSpec((1, d), lambda i, j: (0, 0)),
            pl.BlockSpec((d, tn), lambda i, j: (0, j)),
            pl.BlockSpec((1, tn), lambda i, j: (0, j)),
        ],
        out_specs=pl.BlockSpec((tm, tn), lambda i, j: (i, j)),
        out_shape=jax.ShapeDtypeStruct((t, n), BF16),
        scratch_shapes=[pltpu.VMEM((tm, d), BF16)],
        compiler_params=pltpu.CompilerParams(
            dimension_semantics=("parallel", "arbitrary"), vmem_limit_bytes=VMEM_LIMIT),
        name="in_proj",
    )(x2, g, w, col_scale)


def _dilated_plan(s_len):
    plans = []
    for window, dil in sorted(DILATED_CONFIGS, key=lambda c: -c[1]):
        half = window // (2 * dil)
        sub_len = s_len // dil
        assert s_len % dil == 0 and dil & (dil - 1) == 0
        if sub_len <= 2 * LANES:
            blk, nkeys, nblocks, whole = sub_len, sub_len, dil, True
        else:
            blk, nkeys, nblocks, whole = 2 * half, 4 * half, s_len // (2 * half), False
            assert sub_len % blk == 0 and sub_len // blk >= 2
        assert blk % 8 == 0 and nkeys % LANES == 0
        group = _pick(nblocks, [g for g in (8, 4, 2, 1) if g * blk * nkeys <= 8 * LANES * 2 * LANES])
        plans.append(dict(dil=dil, half=half, sub_len=sub_len, blk=blk, nkeys=nkeys, nblocks=nblocks,
                          whole=whole, group=group))
    return plans


def _dilated_kernel(q_ref, k_ref, v_ref, *rest, plans, pad, pad16):
    bias_refs = rest[:len(plans)]
    o_ref, qf, kf, vf, k16, v16, acc, m_st, l_st = rest[len(plans):]
    s_len = q_ref.shape[1]

    U32 = jnp.uint32
    qf[...] = pltpu.bitcast(q_ref[0], U32)
    for buf, src in ((kf, k_ref), (vf, v_ref)):
        if pad:
            buf[pl.ds(0, pad // 2), :] = jnp.zeros((pad // 2, HEAD_DIM), U32)
            buf[pl.ds((pad + s_len) // 2, pad // 2), :] = jnp.zeros((pad // 2, HEAD_DIM), U32)
        buf[pl.ds(pad // 2, s_len // 2), :] = pltpu.bitcast(src[0], U32)

    def word_rows(start, size, stride):
        return pl.ds(start, size) if stride == 1 else pl.ds(start, size, stride=stride)

    def split_words(w):
        low = pltpu.bitcast(w << 16, F32).astype(BF16)
        high = pltpu.bitcast(w & U32(0xFFFF0000), F32).astype(BF16)
        return low, high
    for buf, src in ((k16, k_ref), (v16, v_ref)):
        buf[pl.ds(0, pad16), :] = jnp.zeros((pad16, HEAD_DIM), BF16)
        buf[pl.ds(pad16 + s_len, pad16), :] = jnp.zeros((pad16, HEAD_DIM), BF16)
        buf[pl.ds(pad16, s_len), :] = src[0]

    for ci, plan in enumerate(plans):
        dil, half, blk, nkeys = plan["dil"], plan["half"], plan["blk"], plan["nkeys"]
        whole, group = plan["whole"], plan["group"]
        shift = dil.bit_length() - 1
        nblk_phase = plan["sub_len"] // blk
        bias_ref = bias_refs[ci]

        def scores(it, ci=ci, dil=dil, half=half, blk=blk, nkeys=nkeys, whole=whole, shift=shift,
                   nblk_phase=nblk_phase, bias_ref=bias_ref):
            if whole:
                phase, sub0, key0, variant = it, 0, 0, 0
            else:
                phase, n = it & (dil - 1), it >> shift
                sub0 = n * blk
                key0 = sub0 - half
                variant = jnp.where(n == 0, 1, jnp.where(n == nblk_phase - 1, 2, 0))
            if dil == 1:
                q_rows = pl.ds(pl.multiple_of(sub0, blk), blk)
                kv_rows = pl.ds(pl.multiple_of(pad16 + key0, half), nkeys)
                operands = [(q_rows, q_ref[0, q_rows, :], k16[kv_rows, :], v16[kv_rows, :])]
            else:
                hd = dil // 2
                qs = split_words(qf[word_rows((phase >> 1) + hd * sub0, blk, hd), :])
                kv_words = word_rows(pad // 2 + (phase >> 1) + hd * key0, nkeys, hd)
                ks, vs = split_words(kf[kv_words, :]), split_words(vf[kv_words, :])
                operands = [(pl.ds(phase + b + dil * sub0, blk, stride=dil), qs[b], ks[b], vs[b])
                            for b in range(2)]
            out = []
            for q_rows, qb, kb, vb in operands:
                old = (m_st[q_rows, :], l_st[q_rows, :], acc[q_rows, :]) if ci > 0 else None
                s = lax.dot_general(qb, kb, (((1,), (1,)), ((), ())), preferred_element_type=F32)
                out.append((q_rows, vb, old, s + bias_ref[0, variant]))
            return out

        def softmax(old, s, ci=ci, blk=blk, nkeys=nkeys):
            m_cur = jnp.max(s, axis=1, keepdims=True)
            if ci == 0:
                m_new = jnp.broadcast_to(m_cur, (blk, LANES))
            else:
                m_new = jnp.maximum(old[0], m_cur)
            p = jnp.exp2(s - jnp.concatenate([m_new] * (nkeys // LANES), axis=1))
            return m_new, jnp.sum(p, axis=1, keepdims=True), p.astype(BF16)

        def values(old, vb, m_new, l_new, p, ci=ci, blk=blk):
            a_new = jnp.dot(p, vb, preferred_element_type=F32)
            if ci == 0:
                return jnp.broadcast_to(l_new, (blk, LANES)), a_new
            alpha = jnp.exp2(old[0] - m_new)
            return alpha * old[1] + l_new, alpha * old[2] + a_new

        per_call = 1 if dil == 1 else 2
        assert group % per_call == 0 and dil % 2 == per_call % 2

        def body(j, carry, scores=scores, softmax=softmax, values=values, group=group, per_call=per_call):
            staged = [blk_ops for g in range(0, group, per_call) for blk_ops in scores(j * group + g)]
            probs = [softmax(old, s) for _, _, old, s in staged]
            for (q_rows, vb, old, _), (m_new, l_new, p) in zip(staged, probs):
                l_new, a_new = values(old, vb, m_new, l_new, p)
                m_st[q_rows, :] = m_new
                l_st[q_rows, :] = l_new
                acc[q_rows, :] = a_new
            return carry

        lax.fori_loop(0, plan["nblocks"] // group, body, 0)

    o_ref[0] = (acc[...] / l_st[...]).astype(o_ref.dtype)


def _dilated_bias(rel_bias, n_heads, plans):
    tables = []
    for plan in plans:
        half, blk, nkeys = plan["half"], plan["blk"], plan["nkeys"]
        key0 = 0 if plan["whole"] else -half
        col = jnp.arange(nkeys)[None, :]
        rel = (col + key0) - jnp.arange(blk)[:, None]
        b = _bias_lookup(rel_bias[:, :n_heads].astype(F32) * LOG2_E,
                         _rel_bucket(rel * plan["dil"], rel_bias.shape[0]))
        band = jnp.abs(rel) <= half
        if plan["whole"]:
            masks = [band]
        else:
            masks = [band, band & (col >= half), band & (col < blk + half)]
        tables.append(jnp.stack([jnp.where(mk[None], b, NEG_INF) for mk in masks], axis=1))
    return tables


def _dilated_attn(proj3, biases, n_heads, plans):
    b, s_len, _ = proj3.shape
    pad = max([p["half"] * p["dil"] for p in plans if not p["whole"] and p["dil"] > 1], default=0)
    pad16 = max([p["half"] for p in plans if p["dil"] == 1], default=16)
    pad16 = -(-pad16 // 16) * 16
    kern = functools.partial(_dilated_kernel, plans=plans, pad=pad, pad16=pad16)
    col = lambda off: (lambda bi, h: (bi, 0, off + h))
    return pl.pallas_call(
        kern,
        grid=(b, n_heads),
        in_specs=[
            pl.BlockSpec((1, s_len, HEAD_DIM), col(0)),
            pl.BlockSpec((1, s_len, HEAD_DIM), col(n_heads)),
            pl.BlockSpec((1, s_len, HEAD_DIM), col(2 * n_heads)),
        ] + [pl.BlockSpec((1,) + t.shape[1:], lambda bi, h: (h, 0, 0, 0)) for t in biases],
        out_specs=pl.BlockSpec((1, s_len, HEAD_DIM), lambda bi, h: (bi, 0, h)),
        out_shape=jax.ShapeDtypeStruct((b, s_len, n_heads * HEAD_DIM), F32),
        scratch_shapes=[
            pltpu.VMEM((s_len // 2, HEAD_DIM), jnp.uint32),
            pltpu.VMEM((s_len // 2 + pad, HEAD_DIM), jnp.uint32),
            pltpu.VMEM((s_len // 2 + pad, HEAD_DIM), jnp.uint32),
            pltpu.VMEM((s_len + 2 * pad16, HEAD_DIM), BF16),
            pltpu.VMEM((s_len + 2 * pad16, HEAD_DIM), BF16),
            pltpu.VMEM((s_len, HEAD_DIM), F32),
            pltpu.VMEM((s_len, LANES), F32),
            pltpu.VMEM((s_len, LANES), F32),
        ],
        compiler_params=pltpu.CompilerParams(
            dimension_semantics=("parallel", "parallel"), vmem_limit_bytes=VMEM_LIMIT),
        name="dilated_attn",
    )(proj3, proj3, proj3, *biases)


def _diff_attn_kernel(q_ref, k_ref, v_ref, bias_ref, lq1_ref, lk1_ref, lq2_ref, lk2_ref, g_ref,
                      o_ref, qs0_ref, qs1_ref, s0_ref, s1_ref, m_ref, acc_ref, vtx_ref, *, tq, tk, heads,
                      lambda_init):
    s_len = k_ref.shape[1]
    nq, nk = s_len // tq, s_len // tk
    qsub, ksub = tq // LANES, tk // LANES
    n_tiles = heads * nq
    tile_shift = nq.bit_length() - 1
    dn_t = (((1,), (1,)), ((), ()))
    s_refs = (s0_ref, s1_ref)
    qs_refs = (qs0_ref, qs1_ref)
    assert nq % 2 == 0 and nq & (nq - 1) == 0

    eye = (lax.broadcasted_iota(jnp.int32, (HEAD_DIM, HEAD_DIM), 0)
           == lax.broadcasted_iota(jnp.int32, (HEAD_DIM, HEAD_DIM), 1)).astype(BF16)
    for hh in range(heads):
        for c in range(nk):
            vc = v_ref[0, pl.ds(c * tk, tk), pl.ds(hh * HEAD_DIM, HEAD_DIM)]
            vtx_ref[hh, pl.ds(0, HEAD_DIM), pl.ds(c * tk, tk)] = lax.dot_general(
                eye, vc, dn_t, preferred_element_type=F32).astype(BF16)
        vtx_ref[hh, pl.ds(HEAD_DIM, ONES_ROWS), :] = jnp.ones((ONES_ROWS, s_len), BF16)
    acc_ref[...] = jnp.zeros(acc_ref.shape, F32)
    lam = (jnp.exp(jnp.sum(lq1_ref[...] * lk1_ref[...], axis=-1, keepdims=True))
           - jnp.exp(jnp.sum(lq2_ref[...] * lk2_ref[...], axis=-1, keepdims=True))
           + lambda_init)

    def split(u):
        if isinstance(u, int):
            hh, t = divmod(u, nq)
            return hh, t, pl.ds(hh * HEAD_DIM, HEAD_DIM), pl.ds(t * tq, tq)
        hh, t = u >> tile_shift, u & (nq - 1)
        return (hh, t, pl.ds(pl.multiple_of(hh * HEAD_DIM, HEAD_DIM), HEAD_DIM),
                pl.ds(pl.multiple_of(t * tq, tq), tq))

    def start_tile(u, slot):
        _, _, lanes, rows = split(u)
        q = q_ref[0, rows, lanes]
        lane = lax.broadcasted_iota(jnp.int32, (tq, HEAD_DIM), 1)
        qs_refs[slot][pl.ds(0, tq), :] = jnp.where(lane < DIFF_QK_DIM, q, 0).astype(BF16)
        qs_refs[slot][pl.ds(tq, tq), :] = jnp.where(lane >= DIFF_QK_DIM, q, 0).astype(BF16)

    def score_chunk(u, c, slot):
        hh, t, lanes, _ = split(u)
        k0 = c * tk
        s = lax.dot_general(k_ref[0, pl.ds(k0, tk), lanes], qs_refs[slot][...], dn_t,
                            preferred_element_type=F32)
        bias_rows = []
        for cc in range(ksub):
            tiles = []
            for a in range(qsub):
                diff = (c * ksub + cc) - (t * qsub + a)
                idx = jnp.minimum(jnp.maximum(diff, -DIFF_BIAS_DMAX), DIFF_BIAS_DMAX) + DIFF_BIAS_DMAX
                tiles.append(bias_ref[hh, idx])
            bias_rows.append(jnp.concatenate(tiles + tiles, axis=1))
        s = s + jnp.concatenate(bias_rows, axis=0)
        s_refs[slot][pl.ds(k0, tk), :] = s
        m_chunk = jnp.max(s.reshape(tk // 8, 8, 2 * tq), axis=0)
        m_ref[0] = m_chunk if c == 0 else jnp.maximum(m_ref[0], m_chunk)

    def finish_scores(slot):
        m_ref[1 + slot] = jnp.broadcast_to(jnp.max(m_ref[0], axis=0, keepdims=True), (8, 2 * tq))

    def value_chunk(u, c, slot):
        hh = split(u)[0]
        k0 = c * tk
        p = jnp.exp2(s_refs[slot][pl.ds(k0, tk), :] - m_ref[1 + slot, pl.ds(0, 1), :]).astype(BF16)
        acc_ref[...] += jnp.dot(vtx_ref[hh, :, pl.ds(k0, tk)], p, preferred_element_type=F32)

    def finish_tile(u):
        _, _, lanes, rows = split(u)
        acc = acc_ref[...]
        o = acc[:HEAD_DIM] / acc[HEAD_DIM:HEAD_DIM + 1]
        od = o[:, :tq] - lam * o[:, tq:]
        ms = jnp.mean(od * od, axis=0, keepdims=True)
        y = jnp.transpose(od * lax.rsqrt(ms + EPS)) * g_ref[...] * (1.0 - lambda_init)
        o_ref[0, rows, lanes] = y.astype(o_ref.dtype)
        acc_ref[...] = jnp.zeros(acc_ref.shape, F32)

    def tile_step(u, slot):
        start_tile(u, slot)
        for c in range(nk):
            score_chunk(u, c, slot)
            if c == nk - 1:
                finish_scores(slot)
            value_chunk(u - 1, c, 1 - slot)
        finish_tile(u - 1)

    start_tile(0, 0)
    for c in range(nk):
        score_chunk(0, c, 0)
    finish_scores(0)

    def pair_step(w, carry):
        tile_step(2 * w + 1, 1)
        tile_step(2 * w + 2, 0)
        return carry

    lax.fori_loop(0, n_tiles // 2 - 1, pair_step, 0)
    tile_step(n_tiles - 1, 1)
    for c in range(nk):
        value_chunk(n_tiles - 1, c, 1)
    finish_tile(n_tiles - 1)


def _diff_bias(rel_bias, n_heads_a, n_heads_b):
    t = jnp.arange(2 * DIFF_BIAS_DMAX + 1) - DIFF_BIAS_DMAX
    rel = (LANES * t[:, None, None] + jnp.arange(LANES)[None, :, None] - jnp.arange(LANES)[None, None, :])
    table = rel_bias[:, n_heads_a:n_heads_a + n_heads_b].astype(F32) * LOG2_E
    return _bias_lookup(table, _rel_bucket(rel, rel_bias.shape[0]))


def _diff_attn(proj3, bias, lams, subln_g, n_heads_a, n_heads_b, lambda_init):
    b, s_len, _ = proj3.shape
    tq = _pick(s_len, (512, 256, 128)) if s_len * 2 <= DIFF_HEAD_ROWS // 2 else _pick(s_len, (256, 128))
    tk = _pick(s_len, (256, 128))
    heads = _pick(n_heads_b, [h for h in (8, 4, 2, 1) if h * s_len <= DIFF_HEAD_ROWS])
    width = heads * HEAD_DIM
    base = 3 * n_heads_a // heads
    assert (3 * n_heads_a) % heads == 0
    kern = functools.partial(_diff_attn_kernel, tq=tq, tk=tk, heads=heads, lambda_init=lambda_init)
    vec = lambda n: pl.BlockSpec((1, n), lambda bi, h: (0, 0))
    col = lambda off: (lambda bi, h: (bi, 0, off + h))
    groups = n_heads_b // heads
    return pl.pallas_call(
        kern,
        grid=(b, groups),
        in_specs=[
            pl.BlockSpec((1, s_len, width), col(base)),
            pl.BlockSpec((1, s_len, width), col(base + groups)),
            pl.BlockSpec((1, s_len, width), col(base + 2 * groups)),
            pl.BlockSpec((heads,) + bias.shape[1:], lambda bi, h: (h, 0, 0, 0)),
            vec(DIFF_QK_DIM), vec(DIFF_QK_DIM), vec(DIFF_QK_DIM), vec(DIFF_QK_DIM),
            vec(HEAD_DIM),
        ],
        out_specs=pl.BlockSpec((1, s_len, width), col(0)),
        out_shape=jax.ShapeDtypeStruct((b, s_len, n_heads_b * HEAD_DIM), BF16),
        scratch_shapes=[
            pltpu.VMEM((2 * tq, HEAD_DIM), BF16),
            pltpu.VMEM((2 * tq, HEAD_DIM), BF16),
            pltpu.VMEM((s_len, 2 * tq), F32),
            pltpu.VMEM((s_len, 2 * tq), F32),
            pltpu.VMEM((3, 8, 2 * tq), F32),
            pltpu.VMEM((HEAD_DIM + ONES_ROWS, 2 * tq), F32),
            pltpu.VMEM((heads, HEAD_DIM + ONES_ROWS, s_len), BF16),
        ],
        compiler_params=pltpu.CompilerParams(
            dimension_semantics=("parallel", "parallel"), vmem_limit_bytes=VMEM_LIMIT),
        name="diff_attn",
    )(proj3, proj3, proj3, bias, *lams, subln_g)


def _out_proj_kernel(oa_ref, ob_ref, x_ref, wa_ref, wb_ref, ga_ref, gp_ref, o_ref):
    oa = _rms(oa_ref[...], ga_ref[...]).astype(BF16)
    y = jnp.dot(oa, wa_ref[...], preferred_element_type=F32)
    y = y + jnp.dot(ob_ref[...], wb_ref[...], preferred_element_type=F32)
    o_ref[...] = x_ref[...] + _rms(y, gp_ref[...])


def _out_proj(oa2, ob2, x2, wa, wb, ga, gp):
    t, d = x2.shape
    wa_w, wb_w = wa.shape[0], wb.shape[0]
    tm = _pick(t, (512, 256, 128))
    const = lambda i: (0, 0)
    row = lambda i: (i, 0)
    return pl.pallas_call(
        _out_proj_kernel,
        grid=(t // tm,),
        in_specs=[
            pl.BlockSpec((tm, wa_w), row),
            pl.BlockSpec((tm, wb_w), row),
            pl.BlockSpec((tm, d), row),
            pl.BlockSpec((wa_w, d), const),
            pl.BlockSpec((wb_w, d), const),
            pl.BlockSpec((1, wa_w), const),
            pl.BlockSpec((1, d), const),
        ],
        out_specs=pl.BlockSpec((tm, d), row),
        out_shape=jax.ShapeDtypeStruct((t, d), F32),
        compiler_params=pltpu.CompilerParams(
            dimension_semantics=("parallel",), vmem_limit_bytes=VMEM_LIMIT),
        name="out_proj",
    )(oa2, ob2, x2, wa, wb, ga, gp)


def _ffn_kernel(x_ref, gpre_ref, wg_ref, wu_ref, wd_ref, gpost_ref, o_ref, h_ref, acc_ref):
    j = pl.program_id(1)

    @pl.when(j == 0)
    def _():
        h_ref[...] = _rms(x_ref[...], gpre_ref[...]).astype(BF16)
        acc_ref[...] = jnp.zeros(acc_ref.shape, F32)

    h = h_ref[...]
    gate = jnp.dot(h, wg_ref[...], preferred_element_type=F32)
    up = jnp.dot(h, wu_ref[...], preferred_element_type=F32)
    f = gate * (1.0 / (1.0 + jnp.exp(-gate))) * up
    acc_ref[...] += jnp.dot(f.astype(BF16), wd_ref[...], preferred_element_type=F32)

    @pl.when(j == pl.num_programs(1) - 1)
    def _():
        o_ref[...] = x_ref[...] + _rms(acc_ref[...], gpost_ref[...])


def _ffn(x2, gpre, wg, wu, wd, gpost):
    t, d = x2.shape
    f = wg.shape[1]
    tm = _pick(t, (512, 256, 128))
    tf = _pick(f, (512, 256, 128))
    return pl.pallas_call(
        _ffn_kernel,
        grid=(t // tm, f // tf),
        in_specs=[
            pl.BlockSpec((tm, d), lambda i, j: (i, 0)),
            pl.BlockSpec((1, d), lambda i, j: (0, 0)),
            pl.BlockSpec((d, tf), lambda i, j: (0, j)),
            pl.BlockSpec((d, tf), lambda i, j: (0, j)),
            pl.BlockSpec((tf, d), lambda i, j: (j, 0)),
            pl.BlockSpec((1, d), lambda i, j: (0, 0)),
        ],
        out_specs=pl.BlockSpec((tm, d), lambda i, j: (i, 0)),
        out_shape=jax.ShapeDtypeStruct((t, d), F32),
        scratch_shapes=[pltpu.VMEM((tm, d), BF16), pltpu.VMEM((tm, d), F32)],
        compiler_params=pltpu.CompilerParams(
            dimension_semantics=("parallel", "arbitrary"), vmem_limit_bytes=VMEM_LIMIT),
        name="ffn",
    )(x2, gpre, wg, wu, wd, gpost)


def _encoder_layer(x, layer, p, bias_a, bias_b):
    b, s_len, d = x.shape
    width = p["w_in"].shape[1] // 6
    n_heads = width // HEAD_DIM
    lambda_init = 0.8 - 0.6 * math.exp(-0.3 * layer)
    row = lambda v: v.reshape(1, -1)

    x2 = x.reshape(b * s_len, d)
    col_scale = (jnp.ones((1, 6 * width), F32).at[:, :width].set(DILATED_Q_SCALE)
                 .at[:, 3 * width:4 * width].set(DIFF_Q_SCALE))
    proj3 = _in_proj(x2, row(p["pre_mix_g"]), p["w_in"], col_scale).reshape(b, s_len, 6 * width)
    oa = _dilated_attn(proj3, bias_a[s_len], n_heads, _dilated_plan(s_len))
    lams = [row(p[k]) for k in ("lambda_q1", "lambda_k1", "lambda_q2", "lambda_k2")]
    ob = _diff_attn(proj3, bias_b, lams, row(p["subln_g"]), n_heads, n_heads, lambda_init)
    x1 = _out_proj(oa.reshape(b * s_len, width), ob.reshape(b * s_len, width), x2,
                   p["w_out"][:width], p["w_out"][width:], row(p["norm_a_g"]), row(p["post_mix_g"]))
    y = _ffn(x1, row(p["pre_ffn_g"]), p["w_gate"], p["w_up"], p["w_down"], row(p["post_ffn_g"]))
    return y.reshape(b, s_len, d)


def kernel(x_prompt, x_sample, pre_mix_g, post_mix_g, pre_ffn_g, post_ffn_g, w_in, w_out, norm_a_g,
           lambda_q1, lambda_k1, lambda_q2, lambda_k2, subln_g, w_gate, w_up, w_down, rel_bias):
    depth = w_in.shape[0]
    n_heads = w_in.shape[2] // 6 // HEAD_DIM
    bias_a = {x.shape[1]: _dilated_bias(rel_bias, n_heads, _dilated_plan(x.shape[1]))
              for x in (x_prompt, x_sample)}
    bias_b = _diff_bias(rel_bias, n_heads, n_heads)
    layers = []
    for layer in range(depth):
        layers.append(dict(
            pre_mix_g=pre_mix_g[layer], post_mix_g=post_mix_g[layer], pre_ffn_g=pre_ffn_g[layer],
            post_ffn_g=post_ffn_g[layer], norm_a_g=norm_a_g[layer], subln_g=subln_g[layer],
            lambda_q1=lambda_q1[layer], lambda_k1=lambda_k1[layer],
            lambda_q2=lambda_q2[layer], lambda_k2=lambda_k2[layer],
            w_in=w_in[layer].astype(BF16), w_out=w_out[layer].astype(BF16),
            w_gate=w_gate[layer].astype(BF16), w_up=w_up[layer].astype(BF16),
            w_down=w_down[layer].astype(BF16)))
    outs = []
    for x in (x_prompt, x_sample):
        for layer, p in enumerate(layers):
            x = _encoder_layer(x, layer, p, bias_a, bias_b)
        outs.append(x)
    return tuple(outs)
```
